```python
import jax, jax.numpy as jnp
from jax import lax
import numpy as np

D_MODEL = 1024
BATCH = 8
SEQ = 4096
DEPTH = 2

SB_HEAD_DIM = 64
SB_HEADS = D_MODEL // 128
SB_WIDTH = SB_HEADS * SB_HEAD_DIM
BLOCK_Q = 128
RET_HEADS = D_MODEL // 256
RET_QK_DIM = 128
RET_V_DIM = 2 * RET_QK_DIM
RET_QK_WIDTH = RET_HEADS * RET_QK_DIM
RET_V_WIDTH = RET_HEADS * RET_V_DIM
RET_CHUNK = 128
ROPE_BASE = 10000.0
SPLIT_SIZES = (SB_WIDTH, SB_WIDTH, SB_WIDTH,
               RET_QK_WIDTH, RET_QK_WIDTH, RET_V_WIDTH,
               RET_V_WIDTH,
               2 * D_MODEL)
IN_COLS = sum(SPLIT_SIZES)
SPLIT_POINTS = tuple(int(v) for v in np.cumsum(SPLIT_SIZES)[:-1])
D_FF = ((8 * D_MODEL // 3 + 127) // 128) * 128
CONV_WIDTH = 3
EPS = 1e-6

kernel_name = "hybrid_stickbreaking_retention_convffn"


def rms_norm(x, g):
    xf = x.astype(jnp.float32)
    y = xf * lax.rsqrt(jnp.mean(xf * xf, axis=-1, keepdims=True) + EPS)
    return (y * g.astype(jnp.float32)).astype(x.dtype)


def rotary(t, pos):
    half = t.shape[-1] // 2
    inv = ROPE_BASE ** (-jnp.arange(half, dtype=jnp.float32) / half)
    ang = pos[:, None] * inv[None, :]
    cos = jnp.cos(ang)[None, :, None, :]
    sin = jnp.sin(ang)[None, :, None, :]
    t = t.astype(jnp.float32)
    t1, t2 = t[..., :half], t[..., half:]
    return jnp.concatenate([t1 * cos - t2 * sin, t1 * sin + t2 * cos], axis=-1)


def stick_breaking_attention(q, k, v):
    S = q.shape[2]
    scale = q.shape[-1] ** -0.5
    outs = []
    for i in range(S // BLOCK_Q):
        end = (i + 1) * BLOCK_Q
        qs = q[:, :, i * BLOCK_Q:end]
        kk, vv = k[:, :, :end], v[:, :, :end]
        z = jnp.einsum('bhqd,bhkd->bhqk', qs, kk).astype(jnp.float32) * scale
        t_idx = i * BLOCK_Q + jnp.arange(BLOCK_Q)[:, None]
        s_idx = jnp.arange(end)[None, :]
        mask = s_idx < t_idx
        log_keep = jnp.where(mask, jax.nn.log_sigmoid(-z), 0.0)
        suffix = lax.cumsum(log_keep, axis=3, reverse=True) - log_keep
        a = jnp.where(mask, jnp.exp(jax.nn.log_sigmoid(z) + suffix), 0.0)
        outs.append(jnp.einsum('bhqk,bhkd->bhqd', a, vv.astype(jnp.float32)))
    return jnp.concatenate(outs, axis=2)


def retention_chunkwise(q, k, v, log_gamma):
    B, H, S, dk = q.shape
    dv = v.shape[-1]
    C = RET_CHUNK
    nC = S // C
    qc = q.reshape(B, H, nC, C, dk)
    kc = k.reshape(B, H, nC, C, dk)
    vc = v.reshape(B, H, nC, C, dv)
    pos = jnp.arange(C, dtype=jnp.float32)
    lg = log_gamma[:, None, None]
    diff = pos[:, None] - pos[None, :]
    decay = jnp.where(diff >= 0, jnp.exp(jnp.maximum(diff, 0.0) * lg), 0.0)
    scores = jnp.einsum('bhcnd,bhcmd->bhcnm', qc, kc) * decay[None, :, None]
    inner = jnp.einsum('bhcnm,bhcme->bhcne', scores, vc)
    zeta = jnp.exp((C - 1 - pos)[None, :] * log_gamma[:, None])
    xi = jnp.exp((pos + 1)[None, :] * log_gamma[:, None])
    chunk_decay = jnp.exp(C * log_gamma)[None, :, None, None]
    kv = jnp.einsum('bhcmd,bhcme->cbhde', kc * zeta[None, :, None, :, None], vc)

    def step(state, kv_c):
        return state * chunk_decay + kv_c, state

    _, prev_states = lax.scan(step, jnp.zeros((B, H, dk, dv), jnp.float32), kv)
    cross = jnp.einsum('bhcnd,cbhde->bhcne', qc, prev_states) * xi[None, :, None, :, None]
    return (inner + cross).reshape(B, H, S, dv)


def head_group_norm(o, g):
    mu = jnp.mean(o, axis=-1, keepdims=True)
    var = jnp.mean(jnp.square(o - mu), axis=-1, keepdims=True)
    y = (o - mu) * lax.rsqrt(var + EPS)
    B, S = o.shape[:2]
    return y.reshape(B, S, -1) * g.astype(jnp.float32)


def hybrid_mixer(x, norm_g, w_in, b_gate, sb_w_out, ret_norm_g, ret_w_out, w_o):
    B, S, _ = x.shape
    h = rms_norm(x, norm_g)
    proj = h @ w_in
    q_sb, k_sb, v_sb, q_r, k_r, v_r, g_r, gates = jnp.split(proj, SPLIT_POINTS, axis=-1)

    def heads(t, n):
        return t.reshape(B, S, n, -1).transpose(0, 2, 1, 3)

    o_sb = stick_breaking_attention(heads(q_sb, SB_HEADS), heads(k_sb, SB_HEADS), heads(v_sb, SB_HEADS))
    o_sb = o_sb.transpose(0, 2, 1, 3).reshape(B, S, SB_WIDTH).astype(x.dtype)
    y_a = o_sb @ sb_w_out

    pos = jnp.arange(S, dtype=jnp.float32)
    qr = rotary(q_r.reshape(B, S, RET_HEADS, RET_QK_DIM), pos).transpose(0, 2, 1, 3)
    kr = (rotary(k_r.reshape(B, S, RET_HEADS, RET_QK_DIM), pos) * RET_QK_DIM ** -0.5).transpose(0, 2, 1, 3)
    vr = heads(v_r, RET_HEADS).astype(jnp.float32)
    log_gamma = jnp.log1p(-jnp.exp2(-5.0 - jnp.arange(RET_HEADS, dtype=jnp.float32)))
    o_r = retention_chunkwise(qr, kr, vr, log_gamma).transpose(0, 2, 1, 3)
    o_r = head_group_norm(o_r, ret_norm_g) * jax.nn.silu(g_r.astype(jnp.float32))
    y_b = o_r.astype(x.dtype) @ ret_w_out

    gate = jax.nn.sigmoid((gates + b_gate).astype(jnp.float32))
    g_a, g_b = gate[..., :D_MODEL], gate[..., D_MODEL:]
    merged = (g_a * y_a.astype(jnp.float32) + g_b * y_b.astype(jnp.float32)).astype(x.dtype)
    return merged @ w_o


def conv_ffn(x, norm_g, w_up, conv_w, conv_b, w_down):
    h = rms_norm(x, norm_g)
    a, b = jnp.split(h @ w_up, 2, axis=-1)
    a = lax.conv_general_dilated(a, conv_w[:, None, :], window_strides=(1,),
                                 padding=[(CONV_WIDTH - 1, 0)],
                                 dimension_numbers=('NWC', 'WIO', 'NWC'),
                                 feature_group_count=D_FF) + conv_b
    return (jax.nn.silu(a) * b) @ w_down


def setup_inputs(seed: int = 0) -> dict:
    key = jax.random.key(seed)
    ks = jax.random.split(key, 16)
    f32 = jnp.float32

    def dense(k, shape, fan_in):
        return jax.random.normal(k, shape, f32) * fan_in ** -0.5

    def gain(k, shape):
        return 1.0 + 0.02 * jax.random.normal(k, shape, f32)

    return {
        "x": jax.random.normal(ks[0], (BATCH, SEQ, D_MODEL), f32),
        "attn_norm": gain(ks[1], (DEPTH, D_MODEL)),
        "w_in": dense(ks[2], (DEPTH, D_MODEL, IN_COLS), D_MODEL),
        "b_gate": 0.02 * jax.random.normal(ks[3], (DEPTH, 2 * D_MODEL), f32),
        "sb_w_out": dense(ks[4], (DEPTH, SB_WIDTH, D_MODEL), SB_WIDTH),
        "ret_norm": gain(ks[5], (DEPTH, RET_V_WIDTH)),
        "ret_w_out": dense(ks[6], (DEPTH, RET_V_WIDTH, D_MODEL), RET_V_WIDTH),
        "w_o": dense(ks[7], (DEPTH, D_MODEL, D_MODEL), D_MODEL),
        "ffn_norm": gain(ks[8], (DEPTH, D_MODEL)),
        "w_up": dense(ks[9], (DEPTH, D_MODEL, 2 * D_FF), D_MODEL),
        "conv_w": dense(ks[10], (DEPTH, CONV_WIDTH, D_FF), CONV_WIDTH),
        "conv_b": 0.02 * jax.random.normal(ks[11], (DEPTH, D_FF), f32),
        "w_down": dense(ks[12], (DEPTH, D_FF, D_MODEL), D_FF),
        "final_norm": gain(ks[13], (D_MODEL,)),
    }


def reference(x, attn_norm, w_in, b_gate, sb_w_out, ret_norm, ret_w_out, w_o,
              ffn_norm, w_up, conv_w, conv_b, w_down, final_norm):
    for l in range(DEPTH):
        x = x + hybrid_mixer(x, attn_norm[l], w_in[l], b_gate[l], sb_w_out[l],
                             ret_norm[l], ret_w_out[l], w_o[l]).astype(x.dtype)
        x = x + conv_ffn(x, ffn_norm[l], w_up[l], conv_w[l], conv_b[l], w_down[l]).astype(x.dtype)
    return rms_norm(x, final_norm)
```

```python
import functools

import jax
import jax.numpy as jnp
import numpy as np
from jax import lax
from jax.experimental import pallas as pl
from jax.experimental.pallas import tpu as pltpu

F32 = jnp.float32
BF16 = jnp.bfloat16

LANES = 128
D_MODEL = 1024
SB_HEAD_DIM = 64
SB_WIDTH = 512
RET_HEADS = 4
RET_QK_DIM = 128
RET_V_DIM = 256
RET_QK_WIDTH = RET_HEADS * RET_QK_DIM
RET_V_WIDTH = RET_HEADS * RET_V_DIM
ROPE_BASE = 10000.0
D_FF = 2816
CONV_WIDTH = 3
EPS = 1e-6

PROJ_BF_COLS = 3 * SB_WIDTH + 2 * RET_QK_WIDTH + RET_V_WIDTH
PROJ_COLS = PROJ_BF_COLS + RET_V_WIDTH
PROJ_CHUNK = 512

RET_CHUNK = 256
SB_BLOCK = 256
HALO = 16

VMEM_LIMIT = 48 * 1024 * 1024


def _rms(x, g):
    ms = jnp.mean(x * x, axis=-1, keepdims=True)
    return x * lax.rsqrt(ms + EPS) * g


def _params(sem):
    return pltpu.CompilerParams(dimension_semantics=sem, vmem_limit_bytes=VMEM_LIMIT)


def _inproj_kernel(x_ref, g_ref, w_ref, cos_ref, sin_ref, obf_ref, og_ref):
    h = _rms(x_ref[...], g_ref[...]).astype(BF16)
    n_bf = PROJ_BF_COLS // PROJ_CHUNK
    for c in range(PROJ_COLS // PROJ_CHUNK):
        lo = c * PROJ_CHUNK
        acc = jnp.dot(h, w_ref[:, lo:lo + PROJ_CHUNK], preferred_element_type=F32)
        if c == 0:
            acc = acc * (SB_HEAD_DIM ** -0.5)
        if c in (3, 4):
            cos = cos_ref[...]
            sin = sin_ref[...]
            for hd in range(PROJ_CHUNK // LANES):
                t = acc[:, hd * LANES:(hd + 1) * LANES]
                o = t * cos + pltpu.roll(t, LANES // 2, axis=1) * sin
                if c == 4:
                    o = o * (RET_QK_DIM ** -0.5)
                obf_ref[:, lo + hd * LANES:lo + (hd + 1) * LANES] = o.astype(BF16)
        elif c < n_bf:
            obf_ref[:, lo:lo + PROJ_CHUNK] = acc.astype(BF16)
        else:
            og_ref[:, lo - PROJ_BF_COLS:lo - PROJ_BF_COLS + PROJ_CHUNK] = acc


def _in_proj(x2, g, w, cosf, sins, *, tm, seq):
    t = x2.shape[0]
    spt = seq // tm
    return pl.pallas_call(
        _inproj_kernel,
        grid=(t // tm,),
        in_specs=[
            pl.BlockSpec((tm, D_MODEL), lambda i: (i, 0)),
            pl.BlockSpec((1, D_MODEL), lambda i: (0, 0)),
            pl.BlockSpec((D_MODEL, PROJ_COLS), lambda i: (0, 0)),
            pl.BlockSpec((tm, LANES), lambda i: (i % spt, 0)),
            pl.BlockSpec((tm, LANES), lambda i: (i % spt, 0)),
        ],
        out_specs=[
            pl.BlockSpec((tm, PROJ_BF_COLS), lambda i: (i, 0)),
            pl.BlockSpec((tm, RET_V_WIDTH), lambda i: (i, 0)),
        ],
        out_shape=[
            jax.ShapeDtypeStruct((t, PROJ_BF_COLS), BF16),
            jax.ShapeDtypeStruct((t, RET_V_WIDTH), F32),
        ],
        compiler_params=_params(("parallel",)),
        name="in_proj",
    )(x2, g, w, cosf, sins)


def _sb_kernel(q_ref, k_ref, v_ref, tri_ref, o_ref, acc_ref, r_ref, *, blk):
    i = pl.program_id(2)
    q = q_ref[...]
    lane = lax.broadcasted_iota(jnp.int32, q.shape, 1)
    zero = jnp.zeros_like(q)
    first = lane < SB_HEAD_DIM
    qq = jnp.concatenate([jnp.where(first, q, zero), jnp.where(first, zero, q)], axis=0)
    tri = tri_ref[...]
    acc_ref[...] = jnp.zeros_like(acc_ref)
    r_ref[...] = jnp.zeros_like(r_ref)

    def block(j, masked):
        start = pl.multiple_of(j * blk, blk)
        kb = k_ref[pl.ds(start, blk), :]
        vb = v_ref[pl.ds(start, blk), :]
        z = lax.dot_general(qq, kb, (((1,), (1,)), ((), ())), preferred_element_type=F32)
        sp = jnp.maximum(z, 0.0) + jnp.log(1.0 + jnp.exp(-jnp.abs(z)))
        if masked:
            row = lax.broadcasted_iota(jnp.int32, z.shape, 0) & (blk - 1)
            col = lax.broadcasted_iota(jnp.int32, z.shape, 1)
            keep = col < row
            sp = jnp.where(keep, sp, 0.0)
        hi = sp.astype(BF16)
        lo = (sp - hi.astype(F32)).astype(BF16)
        cs = (jnp.dot(hi, tri, preferred_element_type=F32)
              + jnp.dot(lo, tri, preferred_element_type=F32))
        r = r_ref[...]
        rr = jnp.concatenate([r] * (blk // LANES), axis=1)
        a = jnp.exp(z - sp - cs - rr)
        if masked:
            a = jnp.where(keep, a, 0.0)
        acc_ref[...] += jnp.dot(a.astype(BF16), vb, preferred_element_type=F32)
        r_ref[...] = r + jnp.sum(sp, axis=-1, keepdims=True)

    block(i, True)

    def body(jj, carry):
        block(i - 1 - jj, False)
        return carry

    lax.fori_loop(0, i, body, 0)
    acc = acc_ref[...]
    o_ref[...] = jnp.where(first, acc[:blk], acc[blk:]).astype(o_ref.dtype)


def _sb_attention(proj_bf, tri, *, batch, seq):
    blk = min(SB_BLOCK, seq)
    nq = seq // blk
    npair = SB_WIDTH // LANES
    t = proj_bf.shape[0]
    return pl.pallas_call(
        functools.partial(_sb_kernel, blk=blk),
        grid=(batch, npair, nq),
        in_specs=[
            pl.BlockSpec((blk, LANES), lambda b, p, i: (b * nq + i, p)),
            pl.BlockSpec((seq, LANES), lambda b, p, i: (b, npair + p)),
            pl.BlockSpec((seq, LANES), lambda b, p, i: (b, 2 * npair + p)),
            pl.BlockSpec((blk, blk), lambda b, p, i: (0, 0)),
        ],
        out_specs=pl.BlockSpec((blk, LANES), lambda b, p, i: (b * nq + i, p)),
        out_shape=jax.ShapeDtypeStruct((t, SB_WIDTH), BF16),
        scratch_shapes=[pltpu.VMEM((2 * blk, LANES), F32), pltpu.VMEM((2 * blk, LANES), F32)],
        compiler_params=_params(("parallel", "parallel", "arbitrary")),
        name="sb_attn",
    )(proj_bf, proj_bf, proj_bf, tri)


def _ret_kernel(q_ref, k_ref, v_ref, g_ref, gain_ref, decay_ref, xi_ref, zeta_ref, cd_ref, o_ref, state_ref):
    c = pl.program_id(2)

    @pl.when(c == 0)
    def _():
        state_ref[...] = jnp.zeros_like(state_ref)

    q = q_ref[...]
    k = k_ref[...]
    v = v_ref[...]
    s = lax.dot_general(q, k, (((1,), (1,)), ((), ())), preferred_element_type=F32) * decay_ref[0]
    inner = jnp.dot(s.astype(BF16), v, preferred_element_type=F32)
    st = state_ref[...]
    cross = jnp.dot(q, st.astype(BF16), preferred_element_type=F32) * xi_ref[0]
    o = inner + cross
    kz = (k.astype(F32) * zeta_ref[0]).astype(BF16)
    kv = lax.dot_general(kz, v, (((0,), (0,)), ((), ())), preferred_element_type=F32)
    state_ref[...] = st * cd_ref[0] + kv

    mu = jnp.mean(o, axis=-1, keepdims=True)
    d = o - mu
    var = jnp.mean(d * d, axis=-1, keepdims=True)
    y = d * lax.rsqrt(var + EPS)
    g = g_ref[...]
    o_ref[...] = (y * gain_ref[...] * (g * jax.nn.sigmoid(g))).astype(o_ref.dtype)


def _retention(proj_bf, g_r, gain, tables, *, batch, seq):
    decay, xi, zeta, cd = tables
    ch = decay.shape[-1]
    nc = seq // ch
    t = proj_bf.shape[0]
    q_off = 3 * SB_WIDTH // RET_QK_DIM
    k_off = q_off + RET_HEADS
    v_off = (3 * SB_WIDTH + 2 * RET_QK_WIDTH) // RET_V_DIM
    return pl.pallas_call(
        _ret_kernel,
        grid=(batch, RET_HEADS, nc),
        in_specs=[
            pl.BlockSpec((ch, RET_QK_DIM), lambda b, h, c: (b * nc + c, q_off + h)),
            pl.BlockSpec((ch, RET_QK_DIM), lambda b, h, c: (b * nc + c, k_off + h)),
            pl.BlockSpec((ch, RET_V_DIM), lambda b, h, c: (b * nc + c, v_off + h)),
            pl.BlockSpec((ch, RET_V_DIM), lambda b, h, c: (b * nc + c, h)),
            pl.BlockSpec((1, RET_V_DIM), lambda b, h, c: (0, h)),
            pl.BlockSpec((1, ch, ch), lambda b, h, c: (h, 0, 0)),
            pl.BlockSpec((1, ch, RET_V_DIM), lambda b, h, c: (h, 0, 0)),
            pl.BlockSpec((1, ch, RET_QK_DIM), lambda b, h, c: (h, 0, 0)),
            pl.BlockSpec((1, 1, RET_V_DIM), lambda b, h, c: (h, 0, 0)),
        ],
        out_specs=pl.BlockSpec((ch, RET_V_DIM), lambda b, h, c: (b * nc + c, h)),
        out_shape=jax.ShapeDtypeStruct((t, RET_V_WIDTH), BF16),
        scratch_shapes=[pltpu.VMEM((RET_QK_DIM, RET_V_DIM), F32)],
        compiler_params=_params(("parallel", "parallel", "arbitrary")),
        name="retention",
    )(proj_bf, proj_bf, proj_bf, g_r, gain, decay, xi, zeta, cd)


def _retention_tables(ch):
    log_gamma = jnp.log1p(-jnp.exp2(-5.0 - jnp.arange(RET_HEADS, dtype=F32)))
    pos = jnp.arange(ch, dtype=F32)
    lg = log_gamma[:, None, None]
    diff = pos[:, None] - pos[None, :]
    decay = jnp.where(diff >= 0, jnp.exp(jnp.maximum(diff, 0.0) * lg), 0.0)
    zeta = jnp.exp((ch - 1 - pos)[None, :] * log_gamma[:, None])
    xi = jnp.exp((pos + 1)[None, :] * log_gamma[:, None])
    cd = jnp.exp(ch * log_gamma)
    xi_b = jnp.broadcast_to(xi[:, :, None], (RET_HEADS, ch, RET_V_DIM))
    zeta_b = jnp.broadcast_to(zeta[:, :, None], (RET_HEADS, ch, RET_QK_DIM))
    cd_b = jnp.broadcast_to(cd[:, None, None], (RET_HEADS, 1, RET_V_DIM))
    return decay, xi_b, zeta_b, cd_b


def _merge_kernel(x_ref, g_ref, wg_ref, bg_ref, osb_ref, wsb_ref, oret_ref, wret_ref, wo_ref, o_ref):
    x = x_ref[...]
    h = _rms(x, g_ref[...]).astype(BF16)
    gate = jax.nn.sigmoid(jnp.dot(h, wg_ref[...], preferred_element_type=F32) + bg_ref[...])
    y_a = jnp.dot(osb_ref[...], wsb_ref[...], preferred_element_type=F32)
    y_b = jnp.dot(oret_ref[...], wret_ref[...], preferred_element_type=F32)
    merged = (gate[:, :D_MODEL] * y_a + gate[:, D_MODEL:] * y_b).astype(BF16)
    o_ref[...] = x + jnp.dot(merged, wo_ref[...], preferred_element_type=F32)


def _merge(x2, g, wg, bg, o_sb, wsb, o_ret, wret, wo, *, tm):
    t = x2.shape[0]
    const = lambda i: (0, 0)
    return pl.pallas_call(
        _merge_kernel,
        grid=(t // tm,),
        in_specs=[
            pl.BlockSpec((tm, D_MODEL), lambda i: (i, 0)),
            pl.BlockSpec((1, D_MODEL), const),
            pl.BlockSpec((D_MODEL, 2 * D_MODEL), const),
            pl.BlockSpec((1, 2 * D_MODEL), const),
            pl.BlockSpec((tm, SB_WIDTH), lambda i: (i, 0)),
            pl.BlockSpec((SB_WIDTH, D_MODEL), const),
            pl.BlockSpec((tm, RET_V_WIDTH), lambda i: (i, 0)),
            pl.BlockSpec((RET_V_WIDTH, D_MODEL), const),
            pl.BlockSpec((D_MODEL, D_MODEL), const),
        ],
        out_specs=pl.BlockSpec((tm, D_MODEL), lambda i: (i, 0)),
        out_shape=jax.ShapeDtypeStruct((t, D_MODEL), F32),
        compiler_params=_params(("parallel",)),
        name="merge",
    )(x2, g, wg, bg, o_sb, wsb, o_ret, wret, wo)


def _ffn_kernel(x_ref, xp_ref, g_ref, wa_ref, wb_ref, cw_ref, cb_ref, wd_ref, fg_ref, o_ref,
                h_ref, hp_ref, acc_ref, abuf_ref, *, tm, tiles_per_seq, n_ff, final):
    i = pl.program_id(0)
    j = pl.program_id(1)

    @pl.when(j == 0)
    def _():
        g = g_ref[...]
        h_ref[...] = _rms(x_ref[...], g).astype(BF16)
        keep = jnp.where(lax.rem(i, tiles_per_seq) == 0, 0.0, 1.0)
        hp_ref[...] = (_rms(xp_ref[...], g) * keep).astype(BF16)

    h = h_ref[...]
    wa = wa_ref[...]
    a = jnp.dot(h, wa, preferred_element_type=F32)
    b = jnp.dot(h, wb_ref[...], preferred_element_type=F32)
    abuf_ref[0:HALO, :] = jnp.dot(hp_ref[...], wa, preferred_element_type=F32)
    abuf_ref[HALO:, :] = a
    cw = cw_ref[...]
    conv = (abuf_ref[HALO - 2:HALO - 2 + tm, :] * cw[0:1, :]
            + abuf_ref[HALO - 1:HALO - 1 + tm, :] * cw[1:2, :]
            + a * cw[2:3, :] + cb_ref[...])
    act = (conv * jax.nn.sigmoid(conv) * b).astype(BF16)
    part = jnp.dot(act, wd_ref[...], preferred_element_type=F32)

    if n_ff > 1:
        @pl.when(j == 0)
        def _():
            acc_ref[...] = part

        @pl.when(jnp.logical_and(j > 0, j < n_ff - 1))
        def _():
            acc_ref[...] += part

    @pl.when(j == n_ff - 1)
    def _():
        y = x_ref[...] + part
        if n_ff > 1:
            y = y + acc_ref[...]
        if final:
            y = _rms(y, fg_ref[...])
        o_ref[...] = y


def _conv_ffn(x2, g, w_up, cw, cb, wd, fg, *, tm, tf, seq, final):
    t = x2.shape[0]
    n_ff = D_FF // tf
    hb = tm // HALO
    kern = functools.partial(_ffn_kernel, tm=tm, tiles_per_seq=seq // tm, n_ff=n_ff, final=final)
    return pl.pallas_call(
        kern,
        grid=(t // tm, n_ff),
        in_specs=[
            pl.BlockSpec((tm, D_MODEL), lambda i, j: (i, 0)),
            pl.BlockSpec((HALO, D_MODEL), lambda i, j: (jnp.maximum(i * hb - 1, 0), 0)),
            pl.BlockSpec((1, D_MODEL), lambda i, j: (0, 0)),
            pl.BlockSpec((D_MODEL, tf), lambda i, j: (0, j)),
            pl.BlockSpec((D_MODEL, tf), lambda i, j: (0, n_ff + j)),
            pl.BlockSpec((CONV_WIDTH, tf), lambda i, j: (0, j)),
            pl.BlockSpec((1, tf), lambda i, j: (0, j)),
            pl.BlockSpec((tf, D_MODEL), lambda i, j: (j, 0)),
            pl.BlockSpec((1, D_MODEL), lambda i, j: (0, 0)),
        ],
        out_specs=pl.BlockSpec((tm, D_MODEL), lambda i, j: (i, 0)),
        out_shape=jax.ShapeDtypeStruct((t, D_MODEL), F32),
        scratch_shapes=[
            pltpu.VMEM((tm, D_MODEL), BF16),
            pltpu.VMEM((HALO, D_MODEL), BF16),
            pltpu.VMEM((tm, D_MODEL), F32),
            pltpu.VMEM((tm + HALO, tf), F32),
        ],
        compiler_params=_params(("parallel", "arbitrary")),
        name="conv_ffn",
    )(x2, x2, g, w_up, w_up, cw, cb, wd, fg)


def kernel(x, attn_norm, w_in, b_gate, sb_w_out, ret_norm, ret_w_out, w_o, ffn_norm, w_up, conv_w, conv_b,
           w_down, final_norm):
    batch, seq, d = x.shape
    assert d == D_MODEL
    depth = w_in.shape[0]
    t = batch * seq
    tm = min(512, seq)
    assert seq % tm == 0 and tm % HALO == 0
    tf = D_FF // 2

    half = RET_QK_DIM // 2
    inv = ROPE_BASE ** (-jnp.arange(half, dtype=F32) / half)
    ang = jnp.arange(seq, dtype=F32)[:, None] * inv[None, :]
    cosf = jnp.concatenate([jnp.cos(ang), jnp.cos(ang)], axis=-1)
    sins = jnp.concatenate([-jnp.sin(ang), jnp.sin(ang)], axis=-1)

    ch = min(RET_CHUNK, seq)
    tables = _retention_tables(ch)
    blk = min(SB_BLOCK, seq)
    idx = jnp.arange(blk)
    tri = (idx[:, None] > idx[None, :]).astype(BF16)

    x2 = x.reshape(t, d)
    for l in range(depth):
        w_in_bf = w_in[l].astype(BF16)
        g_attn = attn_norm[l][None, :]
        proj_bf, g_r = _in_proj(x2, g_attn, w_in_bf[:, :PROJ_COLS], cosf, sins, tm=tm, seq=seq)
        o_sb = _sb_attention(proj_bf, tri, batch=batch, seq=seq)
        o_ret = _retention(proj_bf, g_r, ret_norm[l][None, :], tables, batch=batch, seq=seq)
        x2 = _merge(x2, g_attn, w_in_bf[:, PROJ_COLS:], b_gate[l][None, :], o_sb, sb_w_out[l].astype(BF16),
                    o_ret, ret_w_out[l].astype(BF16), w_o[l].astype(BF16), tm=tm)
        x2 = _conv_ffn(x2, ffn_norm[l][None, :], w_up[l].astype(BF16), conv_w[l], conv_b[l][None, :],
                       w_down[l].astype(BF16), final_norm[None, :], tm=tm, tf=tf, seq=seq,
                       final=(l == depth - 1))
    return x2.reshape(batch, seq, d)
```

```python
import functools

import jax
import jax.numpy as jnp
import numpy as np
from jax import lax
from jax.experimental import pallas as pl
from jax.experimental.pallas import tpu as pltpu

F32 = jnp.float32
BF16 = jnp.bfloat16

LANES = 128
D_MODEL = 1024
SB_HEAD_DIM = 64
SB_WIDTH = 512
RET_HEADS = 4
RET_QK_DIM = 128
RET_V_DIM = 256
RET_QK_WIDTH = RET_HEADS * RET_QK_DIM
RET_V_WIDTH = RET_HEADS * RET_V_DIM
ROPE_BASE = 10000.0
D_FF = 2816
CONV_WIDTH = 3
EPS = 1e-6

PROJ_BF_COLS = 3 * SB_WIDTH + 2 * RET_QK_WIDTH + RET_V_WIDTH
PROJ_COLS = PROJ_BF_COLS + RET_V_WIDTH
PROJ_CHUNK = 512

RET_CHUNK = 256
SB_BLOCK = 256
SB_CHAINS = 4
SB_SKIP = 104.0
HALO = 16

VMEM_LIMIT = 48 * 1024 * 1024


def _rms(x, g):
    ms = jnp.mean(x * x, axis=-1, keepdims=True)
    return x * lax.rsqrt(ms + EPS) * g


def _params(sem):
    return pltpu.CompilerParams(dimension_semantics=sem, vmem_limit_bytes=VMEM_LIMIT)


def _inproj_kernel(x_ref, g_ref, w_ref, cos_ref, sin_ref, obf_ref, og_ref):
    h = _rms(x_ref[...], g_ref[...]).astype(BF16)
    n_bf = PROJ_BF_COLS // PROJ_CHUNK
    for c in range(PROJ_COLS // PROJ_CHUNK):
        lo = c * PROJ_CHUNK
        acc = jnp.dot(h, w_ref[:, lo:lo + PROJ_CHUNK], preferred_element_type=F32)
        if c == 0:
            acc = acc * (SB_HEAD_DIM ** -0.5)
        if c in (3, 4):
            cos = cos_ref[...]
            sin = sin_ref[...]
            for hd in range(PROJ_CHUNK // LANES):
                t = acc[:, hd * LANES:(hd + 1) * LANES]
                o = t * cos + pltpu.roll(t, LANES // 2, axis=1) * sin
                if c == 4:
                    o = o * (RET_QK_DIM ** -0.5)
                obf_ref[:, lo + hd * LANES:lo + (hd + 1) * LANES] = o.astype(BF16)
        elif c < n_bf:
            obf_ref[:, lo:lo + PROJ_CHUNK] = acc.astype(BF16)
        else:
            og_ref[:, lo - PROJ_BF_COLS:lo - PROJ_BF_COLS + PROJ_CHUNK] = acc


def _in_proj(x2, g, w, cosf, sins, *, tm, seq):
    t = x2.shape[0]
    spt = seq // tm
    return pl.pallas_call(
        _inproj_kernel,
        grid=(t // tm,),
        in_specs=[
            pl.BlockSpec((tm, D_MODEL), lambda i: (i, 0)),
            pl.BlockSpec((1, D_MODEL), lambda i: (0, 0)),
            pl.BlockSpec((D_MODEL, PROJ_COLS), lambda i: (0, 0)),
            pl.BlockSpec((tm, LANES), lambda i: (i % spt, 0)),
            pl.BlockSpec((tm, LANES), lambda i: (i % spt, 0)),
        ],
        out_specs=[
            pl.BlockSpec((tm, PROJ_BF_COLS), lambda i: (i, 0)),
            pl.BlockSpec((tm, RET_V_WIDTH), lambda i: (i, 0)),
        ],
        out_shape=[
            jax.ShapeDtypeStruct((t, PROJ_BF_COLS), BF16),
            jax.ShapeDtypeStruct((t, RET_V_WIDTH), F32),
        ],
        compiler_params=_params(("parallel",)),
        name="in_proj",
    )(x2, g, w, cosf, sins)


def _sb_kernel(q_ref, k_ref, v_ref, tri_ref, o_ref, qq_ref, acc_ref, r_ref, *, blk, n_chain):
    i = pl.program_id(2)
    q = q_ref[...]
    lane = lax.broadcasted_iota(jnp.int32, q.shape, 1)
    zero = jnp.zeros_like(q)
    first = lane < SB_HEAD_DIM
    qq_ref[0:blk, :] = jnp.where(first, q, zero)
    qq_ref[blk:, :] = jnp.where(first, zero, q)
    acc_ref[...] = jnp.zeros_like(acc_ref)
    r_ref[...] = jnp.zeros_like(r_ref)
    rows = 2 * blk // n_chain

    def block(j, masked):
        start = pl.multiple_of(j * blk, blk)
        kb = k_ref[pl.ds(start, blk), :]
        vb = v_ref[pl.ds(start, blk), :]
        tri = tri_ref[...]
        chains = range(n_chain)
        sls = [pl.ds(c * rows, rows) for c in chains]
        zs = [lax.dot_general(qq_ref[sls[c], :], kb, (((1,), (1,)), ((), ())),
                              preferred_element_type=F32) for c in chains]
        keeps, sps, css = [], [], []
        for c in chains:
            z = zs[c]
            sp = jnp.maximum(z, 0.0) + jnp.log(1.0 + jnp.exp(-jnp.abs(z)))
            if masked:
                row = (lax.broadcasted_iota(jnp.int32, z.shape, 0) + c * rows) & (blk - 1)
                col = lax.broadcasted_iota(jnp.int32, z.shape, 1)
                keeps.append(col < row)
                sp = jnp.where(keeps[c], sp, 0.0)
            hi = sp.astype(BF16)
            lo = (sp - hi.astype(F32)).astype(BF16)
            sps.append(sp)
            css.append(jnp.dot(hi, tri, preferred_element_type=F32)
                       + jnp.dot(lo, tri, preferred_element_type=F32))
        for c in chains:
            r = r_ref[sls[c], :]
            rr = jnp.concatenate([r] * (blk // LANES), axis=1)
            a = jnp.exp(zs[c] - sps[c] - css[c] - rr)
            if masked:
                a = jnp.where(keeps[c], a, 0.0)
            acc_ref[sls[c], :] += jnp.dot(a.astype(BF16), vb, preferred_element_type=F32)
            r_ref[sls[c], :] = r + jnp.sum(sps[c], axis=-1, keepdims=True)

    block(i, True)

    def cond(carry):
        j, rmin = carry
        return jnp.logical_and(j >= 0, rmin < SB_SKIP)

    def body(carry):
        j, _ = carry
        block(j, False)
        return j - 1, jnp.min(r_ref[...])

    lax.while_loop(cond, body, (i - 1, jnp.min(r_ref[...])))
    acc = acc_ref[...]
    o_ref[...] = jnp.where(first, acc[:blk], acc[blk:]).astype(o_ref.dtype)


def _sb_attention(proj_bf, tri, *, batch, seq):
    blk = min(SB_BLOCK, seq)
    nq = seq // blk
    npair = SB_WIDTH // LANES
    t = proj_bf.shape[0]
    return pl.pallas_call(
        functools.partial(_sb_kernel, blk=blk, n_chain=SB_CHAINS),
        grid=(batch, npair, nq),
        in_specs=[
            pl.BlockSpec((blk, LANES), lambda b, p, i: (b * nq + i, p)),
            pl.BlockSpec((seq, LANES), lambda b, p, i: (b, npair + p)),
            pl.BlockSpec((seq, LANES), lambda b, p, i: (b, 2 * npair + p)),
            pl.BlockSpec((blk, blk), lambda b, p, i: (0, 0)),
        ],
        out_specs=pl.BlockSpec((blk, LANES), lambda b, p, i: (b * nq + i, p)),
        out_shape=jax.ShapeDtypeStruct((t, SB_WIDTH), BF16),
        scratch_shapes=[pltpu.VMEM((2 * blk, LANES), BF16), pltpu.VMEM((2 * blk, LANES), F32),
                        pltpu.VMEM((2 * blk, LANES), F32)],
        compiler_params=_params(("parallel", "parallel", "arbitrary")),
        name="sb_attn",
    )(proj_bf, proj_bf, proj_bf, tri)


def _ret_kernel(q_ref, k_ref, v0_ref, v1_ref, g_ref, gain_ref, decay_ref, xi_ref, zeta_ref, cd_ref, o_ref,
                state_ref):
    c = pl.program_id(1)

    @pl.when(c == 0)
    def _():
        state_ref[...] = jnp.zeros_like(state_ref)

    heads = range(RET_HEADS)
    v_refs = (v0_ref, v1_ref)
    per = RET_HEADS // len(v_refs)

    def qk(ref, h):
        return ref[:, h * RET_QK_DIM:(h + 1) * RET_QK_DIM]

    def vh(h):
        return v_refs[h // per][:, (h % per) * RET_V_DIM:(h % per + 1) * RET_V_DIM]

    def vsl(h):
        return slice(h * RET_V_DIM, (h + 1) * RET_V_DIM)

    s = [lax.dot_general(qk(q_ref, h), qk(k_ref, h), (((1,), (1,)), ((), ())), preferred_element_type=F32)
         for h in heads]
    cross = [jnp.dot(qk(q_ref, h), state_ref[h].astype(BF16), preferred_element_type=F32) for h in heads]
    kv = []
    for h in heads:
        kz = (qk(k_ref, h).astype(F32) * zeta_ref[h]).astype(BF16)
        kv.append(lax.dot_general(kz, vh(h), (((0,), (0,)), ((), ())), preferred_element_type=F32))
    inner = [jnp.dot((s[h] * decay_ref[h]).astype(BF16), vh(h), preferred_element_type=F32) for h in heads]
    for h in heads:
        state_ref[h] = state_ref[h] * cd_ref[h] + kv[h]
    for h in heads:
        o = inner[h] + cross[h] * xi_ref[h]
        mu = jnp.mean(o, axis=-1, keepdims=True)
        d = o - mu
        var = jnp.mean(d * d, axis=-1, keepdims=True)
        y = d * lax.rsqrt(var + EPS)
        g = g_ref[:, vsl(h)]
        o_ref[:, vsl(h)] = (y * gain_ref[:, vsl(h)] * (g * jax.nn.sigmoid(g))).astype(o_ref.dtype)


def _retention(proj_bf, g_r, gain, tables, *, batch, seq):
    decay, xi, zeta, cd = tables
    ch = decay.shape[-1]
    nc = seq // ch
    t = proj_bf.shape[0]
    q_off = 3 * SB_WIDTH // RET_QK_WIDTH
    v_half = RET_V_WIDTH // 2
    v_off = (3 * SB_WIDTH + 2 * RET_QK_WIDTH) // v_half
    const3 = lambda b, c: (0, 0, 0)
    return pl.pallas_call(
        _ret_kernel,
        grid=(batch, nc),
        in_specs=[
            pl.BlockSpec((ch, RET_QK_WIDTH), lambda b, c: (b * nc + c, q_off)),
            pl.BlockSpec((ch, RET_QK_WIDTH), lambda b, c: (b * nc + c, q_off + 1)),
            pl.BlockSpec((ch, v_half), lambda b, c: (b * nc + c, v_off)),
            pl.BlockSpec((ch, v_half), lambda b, c: (b * nc + c, v_off + 1)),
            pl.BlockSpec((ch, RET_V_WIDTH), lambda b, c: (b * nc + c, 0)),
            pl.BlockSpec((1, RET_V_WIDTH), lambda b, c: (0, 0)),
            pl.BlockSpec((RET_HEADS, ch, ch), const3),
            pl.BlockSpec((RET_HEADS, ch, RET_V_DIM), const3),
            pl.BlockSpec((RET_HEADS, ch, RET_QK_DIM), const3),
            pl.BlockSpec((RET_HEADS, 1, RET_V_DIM), const3),
        ],
        out_specs=pl.BlockSpec((ch, RET_V_WIDTH), lambda b, c: (b * nc + c, 0)),
        out_shape=jax.ShapeDtypeStruct((t, RET_V_WIDTH), BF16),
        scratch_shapes=[pltpu.VMEM((RET_HEADS, RET_QK_DIM, RET_V_DIM), F32)],
        compiler_params=_params(("parallel", "arbitrary")),
        name="retention",
    )(proj_bf, proj_bf, proj_bf, proj_bf, g_r, gain, decay, xi, zeta, cd)


def _retention_tables(ch):
    log_gamma = jnp.log1p(-jnp.exp2(-5.0 - jnp.arange(RET_HEADS, dtype=F32)))
    pos = jnp.arange(ch, dtype=F32)
    lg = log_gamma[:, None, None]
    diff = pos[:, None] - pos[None, :]
    decay = jnp.where(diff >= 0, jnp.exp(jnp.maximum(diff, 0.0) * lg), 0.0)
    zeta = jnp.exp((ch - 1 - pos)[None, :] * log_gamma[:, None])
    xi = jnp.exp((pos + 1)[None, :] * log_gamma[:, None])
    cd = jnp.exp(ch * log_gamma)
    xi_b = jnp.broadcast_to(xi[:, :, None], (RET_HEADS, ch, RET_V_DIM))
    zeta_b = jnp.broadcast_to(zeta[:, :, None], (RET_HEADS, ch, RET_QK_DIM))
    cd_b = jnp.broadcast_to(cd[:, None, None], (RET_HEADS, 1, RET_V_DIM))
    return decay, xi_b, zeta_b, cd_b


def _merge_kernel(x_ref, g_ref, wg_ref, bg_ref, osb_ref, wsb_ref, oret_ref, wret_ref, wo_ref, o_ref):
    x = x_ref[...]
    h = _rms(x, g_ref[...]).astype(BF16)
    gate = jax.nn.sigmoid(jnp.dot(h, wg_ref[...], preferred_element_type=F32) + bg_ref[...])
    y_a = jnp.dot(osb_ref[...], wsb_ref[...], preferred_element_type=F32)
    y_b = jnp.dot(oret_ref[...], wret_ref[...], preferred_element_type=F32)
    merged = (gate[:, :D_MODEL] * y_a + gate[:, D_MODEL:] * y_b).astype(BF16)
    o_ref[...] = x + jnp.dot(merged, wo_ref[...], preferred_element_type=F32)


def _merge(x2, g, wg, bg, o_sb, wsb, o_ret, wret, wo, *, tm):
    t = x2.shape[0]
    const = lambda i: (0, 0)
    return pl.pallas_call(
        _merge_kernel,
        grid=(t // tm,),
        in_specs=[
            pl.BlockSpec((tm, D_MODEL), lambda i: (i, 0)),
            pl.BlockSpec((1, D_MODEL), const),
            pl.BlockSpec((D_MODEL, 2 * D_MODEL), const),
            pl.BlockSpec((1, 2 * D_MODEL), const),
            pl.BlockSpec((tm, SB_WIDTH), lambda i: (i, 0)),
            pl.BlockSpec((SB_WIDTH, D_MODEL), const),
            pl.BlockSpec((tm, RET_V_WIDTH), lambda i: (i, 0)),
            pl.BlockSpec((RET_V_WIDTH, D_MODEL), const),
            pl.BlockSpec((D_MODEL, D_MODEL), const),
        ],
        out_specs=pl.BlockSpec((tm, D_MODEL), lambda i: (i, 0)),
        out_shape=jax.ShapeDtypeStruct((t, D_MODEL), F32),
        compiler_params=_params(("parallel",)),
        name="merge",
    )(x2, g, wg, bg, o_sb, wsb, o_ret, wret, wo)


def _ffn_kernel(x_ref, xp_ref, g_ref, wa_ref, wb_ref, cw_ref, cb_ref, wd_ref, fg_ref, o_ref,
                h_ref, hp_ref, acc_ref, abuf_ref, *, tm, tiles_per_seq, n_ff, final):
    i = pl.program_id(0)
    j = pl.program_id(1)

    @pl.when(j == 0)
    def _():
        g = g_ref[...]
        h_ref[...] = _rms(x_ref[...], g).astype(BF16)
        keep = jnp.where(lax.rem(i, tiles_per_seq) == 0, 0.0, 1.0)
        hp_ref[...] = (_rms(xp_ref[...], g) * keep).astype(BF16)

    h = h_ref[...]
    wa = wa_ref[...]
    a = jnp.dot(h, wa, preferred_element_type=F32)
    b = jnp.dot(h, wb_ref[...], preferred_element_type=F32)
    abuf_ref[0:HALO, :] = jnp.dot(hp_ref[...], wa, preferred_element_type=F32)
    abuf_ref[HALO:, :] = a
    cw = cw_ref[...]
    conv = (abuf_ref[HALO - 2:HALO - 2 + tm, :] * cw[0:1, :]
            + abuf_ref[HALO - 1:HALO - 1 + tm, :] * cw[1:2, :]
            + a * cw[2:3, :] + cb_ref[...])
    act = (conv * jax.nn.sigmoid(conv) * b).astype(BF16)
    part = jnp.dot(act, wd_ref[...], preferred_element_type=F32)

    if n_ff > 1:
        @pl.when(j == 0)
        def _():
            acc_ref[...] = part

        @pl.when(jnp.logical_and(j > 0, j < n_ff - 1))
        def _():
            acc_ref[...] += part

    @pl.when(j == n_ff - 1)
    def _():
        y = x_ref[...] + part
        if n_ff > 1:
            y = y + acc_ref[...]
        if final:
            y = _rms(y, fg_ref[...])
        o_ref[...] = y


def _conv_ffn(x2, g, w_up, cw, cb, wd, fg, *, tm, tf, seq, final):
    t = x2.shape[0]
    n_ff = D_FF // tf
    hb = tm // HALO
    kern = functools.partial(_ffn_kernel, tm=tm, tiles_per_seq=seq // tm, n_ff=n_ff, final=final)
    return pl.pallas_call(
        kern,
        grid=(t // tm, n_ff),
        in_specs=[
            pl.BlockSpec((tm, D_MODEL), lambda i, j: (i, 0)),
            pl.BlockSpec((HALO, D_MODEL), lambda i, j: (jnp.maximum(i * hb - 1, 0), 0)),
            pl.BlockSpec((1, D_MODEL), lambda i, j: (0, 0)),
            pl.BlockSpec((D_MODEL, tf), lambda i, j: (0, j)),
            pl.BlockSpec((D_MODEL, tf), lambda i, j: (0, n_ff + j)),
            pl.BlockSpec((CONV_WIDTH, tf), lambda i, j: (0, j)),
            pl.BlockSpec((1, tf), lambda i, j: (0, j)),
            pl.BlockSpec((tf, D_MODEL), lambda i, j: (j, 0)),
            pl.BlockSpec((1, D_MODEL), lambda i, j: (0, 0)),
        ],
        out_specs=pl.BlockSpec((tm, D_MODEL), lambda i, j: (i, 0)),
        out_shape=jax.ShapeDtypeStruct((t, D_MODEL), F32),
        scratch_shapes=[
            pltpu.VMEM((tm, D_MODEL), BF16),
            pltpu.VMEM((HALO, D_MODEL), BF16),
            pltpu.VMEM((tm, D_MODEL), F32),
            pltpu.VMEM((tm + HALO, tf), F32),
        ],
        compiler_params=_params(("parallel", "arbitrary")),
        name="conv_ffn",
    )(x2, x2, g, w_up, w_up, cw, cb, wd, fg)


def kernel(x, attn_norm, w_in, b_gate, sb_w_out, ret_norm, ret_w_out, w_o, ffn_norm, w_up, conv_w, conv_b,
           w_down, final_norm):
    batch, seq, d = x.shape
    assert d == D_MODEL
    depth = w_in.shape[0]
    t = batch * seq
    tm = min(512, seq)
    assert seq % tm == 0 and tm % HALO == 0
    tf = D_FF // 2

    half = RET_QK_DIM // 2
    inv = ROPE_BASE ** (-jnp.arange(half, dtype=F32) / half)
    ang = jnp.arange(seq, dtype=F32)[:, None] * inv[None, :]
    cosf = jnp.concatenate([jnp.cos(ang), jnp.cos(ang)], axis=-1)
    sins = jnp.concatenate([-jnp.sin(ang), jnp.sin(ang)], axis=-1)

    ch = min(RET_CHUNK, seq)
    tables = _retention_tables(ch)
    blk = min(SB_BLOCK, seq)
    idx = jnp.arange(blk)
    tri = (idx[:, None] > idx[None, :]).astype(BF16)

    x2 = x.reshape(t, d)
    for l in range(depth):
        w_in_bf = w_in[l].astype(BF16)
        g_attn = attn_norm[l][None, :]
        proj_bf, g_r = _in_proj(x2, g_attn, w_in_bf[:, :PROJ_COLS], cosf, sins, tm=tm, seq=seq)
        o_sb = _sb_attention(proj_bf, tri, batch=batch, seq=seq)
        o_ret = _retention(proj_bf, g_r, ret_norm[l][None, :], tables, batch=batch, seq=seq)
        x2 = _merge(x2, g_attn, w_in_bf[:, PROJ_COLS:], b_gate[l][None, :], o_sb, sb_w_out[l].astype(BF16),
                    o_ret, ret_w_out[l].astype(BF16), w_o[l].astype(BF16), tm=tm)
        x2 = _conv_ffn(x2, ffn_norm[l][None, :], w_up[l].astype(BF16), conv_w[l], conv_b[l][None, :],
                       w_down[l].astype(BF16), final_norm[None, :], tm=tm, tf=tf, seq=seq,
                       final=(l == depth - 1))
    return x2.reshape(batch, seq, d)
```

```python
import functools

import jax
import jax.numpy as jnp
import numpy as np
from jax import lax
from jax.experimental import pallas as pl
from jax.experimental.pallas import tpu as pltpu

F32 = jnp.float32
BF16 = jnp.bfloat16

LANES = 128
D_MODEL = 1024
SB_HEAD_DIM = 64
SB_WIDTH = 512
RET_HEADS = 4
RET_QK_DIM = 128
RET_V_DIM = 256
RET_QK_WIDTH = RET_HEADS * RET_QK_DIM
RET_V_WIDTH = RET_HEADS * RET_V_DIM
ROPE_BASE = 10000.0
D_FF = 2816
CONV_WIDTH = 3
EPS = 1e-6

PROJ_BF_COLS = 3 * SB_WIDTH + 2 * RET_QK_WIDTH + RET_V_WIDTH
PROJ_COLS = PROJ_BF_COLS + RET_V_WIDTH
PROJ_CHUNK = 512

RET_CHUNK = 256
SB_BLOCK = 256
SB_CHAINS = 2
SB_DEPTH = 2
SB_SKIP = 104.0
SB_MASKED = 1e30
HALO = 16
FF_CHUNK = 512

VMEM_LIMIT = 48 * 1024 * 1024


def _rms(x, g):
    ms = jnp.mean(x * x, axis=-1, keepdims=True)
    return x * lax.rsqrt(ms + EPS) * g


def _params(sem):
    return pltpu.CompilerParams(dimension_semantics=sem, vmem_limit_bytes=VMEM_LIMIT)


def _inproj_kernel(x_ref, g_ref, w_ref, cos_ref, sin_ref, obf_ref, og_ref):
    h = _rms(x_ref[...], g_ref[...]).astype(BF16)
    n_bf = PROJ_BF_COLS // PROJ_CHUNK
    for c in range(PROJ_COLS // PROJ_CHUNK):
        lo = c * PROJ_CHUNK
        acc = jnp.dot(h, w_ref[:, lo:lo + PROJ_CHUNK], preferred_element_type=F32)
        if c == 0:
            acc = acc * (SB_HEAD_DIM ** -0.5)
        if c in (3, 4):
            cos = cos_ref[...]
            sin = sin_ref[...]
            for hd in range(PROJ_CHUNK // LANES):
                t = acc[:, hd * LANES:(hd + 1) * LANES]
                o = t * cos + pltpu.roll(t, LANES // 2, axis=1) * sin
                if c == 4:
                    o = o * (RET_QK_DIM ** -0.5)
                obf_ref[:, lo + hd * LANES:lo + (hd + 1) * LANES] = o.astype(BF16)
        elif c < n_bf:
            obf_ref[:, lo:lo + PROJ_CHUNK] = acc.astype(BF16)
        else:
            og_ref[:, lo - PROJ_BF_COLS:lo - PROJ_BF_COLS + PROJ_CHUNK] = acc


def _in_proj(x2, g, w, cosf, sins, *, tm, seq):
    t = x2.shape[0]
    spt = seq // tm
    return pl.pallas_call(
        _inproj_kernel,
        grid=(t // tm,),
        in_specs=[
            pl.BlockSpec((tm, D_MODEL), lambda i: (i, 0)),
            pl.BlockSpec((1, D_MODEL), lambda i: (0, 0)),
            pl.BlockSpec((D_MODEL, PROJ_COLS), lambda i: (0, 0)),
            pl.BlockSpec((tm, LANES), lambda i: (i % spt, 0)),
            pl.BlockSpec((tm, LANES), lambda i: (i % spt, 0)),
        ],
        out_specs=[
            pl.BlockSpec((tm, PROJ_BF_COLS), lambda i: (i, 0)),
            pl.BlockSpec((tm, RET_V_WIDTH), lambda i: (i, 0)),
        ],
        out_shape=[
            jax.ShapeDtypeStruct((t, PROJ_BF_COLS), BF16),
            jax.ShapeDtypeStruct((t, RET_V_WIDTH), F32),
        ],
        compiler_params=_params(("parallel",)),
        name="in_proj",
    )(x2, g, w, cosf, sins)


def _sb_kernel(q_ref, k_ref, v_ref, tri_ref, o_ref, qq_ref, acc_ref, r_ref, *, blk, n_chain):
    i = pl.program_id(2)
    q = q_ref[...]
    lane = lax.broadcasted_iota(jnp.int32, q.shape, 1)
    zero = jnp.zeros_like(q)
    first = lane < SB_HEAD_DIM
    qq_ref[0:blk, :] = jnp.where(first, q, zero)
    qq_ref[blk:, :] = jnp.where(first, zero, q)
    rows = 2 * blk // n_chain
    chains = range(n_chain)
    sls = [pl.ds(c * rows, rows) for c in chains]

    def sweep(blocks, fresh):
        tri2 = tri_ref[...]
        kvs = []
        for j, _ in blocks:
            start = pl.multiple_of(j * blk, blk)
            kvs.append((k_ref[pl.ds(start, blk), :], v_ref[pl.ds(start, blk), :]))
        units = [(b, c) for b in range(len(blocks)) for c in chains]
        n = len(units)
        zs, st = {}, {}
        r = [None if fresh else r_ref[sls[c], :] for c in chains]
        o = [None] * n_chain

        def scores(u):
            b, c = units[u]
            z = lax.dot_general(qq_ref[sls[c], :], kvs[b][0], (((1,), (1,)), ((), ())),
                                preferred_element_type=F32)
            if blocks[b][1]:
                row = (lax.broadcasted_iota(jnp.int32, z.shape, 0) + c * rows) & (blk - 1)
                col = lax.broadcasted_iota(jnp.int32, z.shape, 1)
                z = jnp.where(col < row, z, -SB_MASKED)
            zs[u] = z

        def softplus_cumsum(u):
            z = zs.pop(u)
            sp = jnp.maximum(z, 0.0) + jnp.log(1.0 + jnp.exp(-jnp.abs(z)))
            hi = sp.astype(BF16)
            lo = (sp - hi.astype(F32)).astype(BF16)
            cs = jnp.dot(jnp.concatenate([hi, lo], axis=1), tri2,
                         preferred_element_type=F32)
            st[u] = (z - sp, cs, jnp.sum(sp, axis=-1, keepdims=True))

        def weights_values(u):
            b, c = units[u]
            w, cs, rs = st.pop(u)
            e = w - cs
            if r[c] is not None:
                rc = r[c]
                e = e - (rc if rc.shape[1] == 1 else jnp.concatenate([rc] * (blk // LANES), axis=1))
            part = jnp.dot(jnp.exp(e).astype(BF16), kvs[b][1], preferred_element_type=F32)
            o[c] = part if o[c] is None else o[c] + part
            r[c] = rs if r[c] is None else r[c] + rs

        for u in range(min(SB_DEPTH, n)):
            scores(u)
        for u in range(n):
            softplus_cumsum(u)
            if u + SB_DEPTH < n:
                scores(u + SB_DEPTH)
            if u >= 1:
                weights_values(u - 1)
        weights_values(n - 1)
        for c in chains:
            acc_ref[sls[c], :] = o[c] if fresh else acc_ref[sls[c], :] + o[c]
            r_ref[sls[c], :] = jnp.broadcast_to(r[c], (rows, LANES))

    @pl.when(i == 0)
    def _():
        sweep([(i, True)], True)

    @pl.when(i > 0)
    def _():
        sweep([(i, True), (i - 1, False)], True)

    def cond(carry):
        j, rmin = carry
        return jnp.logical_and(j >= 0, rmin < SB_SKIP)

    def body(carry):
        j, _ = carry
        sweep([(j, False)], False)
        return j - 1, jnp.min(r_ref[...])

    lax.while_loop(cond, body, (i - 2, jnp.min(r_ref[...])))
    acc = acc_ref[...]
    o_ref[...] = jnp.where(first, acc[:blk], acc[blk:]).astype(o_ref.dtype)


def _sb_attention(proj_bf, tri, *, batch, seq):
    blk = min(SB_BLOCK, seq)
    nq = seq // blk
    npair = SB_WIDTH // LANES
    t = proj_bf.shape[0]
    return pl.pallas_call(
        functools.partial(_sb_kernel, blk=blk, n_chain=SB_CHAINS),
        grid=(batch, npair, nq),
        in_specs=[
            pl.BlockSpec((blk, LANES), lambda b, p, i: (b * nq + i, p)),
            pl.BlockSpec((seq, LANES), lambda b, p, i: (b, npair + p)),
            pl.BlockSpec((seq, LANES), lambda b, p, i: (b, 2 * npair + p)),
            pl.BlockSpec((2 * blk, blk), lambda b, p, i: (0, 0)),
        ],
        out_specs=pl.BlockSpec((blk, LANES), lambda b, p, i: (b * nq + i, p)),
        out_shape=jax.ShapeDtypeStruct((t, SB_WIDTH), BF16),
        scratch_shapes=[pltpu.VMEM((2 * blk, LANES), BF16), pltpu.VMEM((2 * blk, LANES), F32),
                        pltpu.VMEM((2 * blk, LANES), F32)],
        compiler_params=_params(("parallel", "parallel", "arbitrary")),
        name="sb_attn",
    )(proj_bf, proj_bf, proj_bf, tri)


def _ret_kernel(q_ref, k_ref, v0_ref, v1_ref, g_ref, gain_ref, decay_ref, xi_ref, zeta_ref, cd_ref, o_ref,
                state_ref):
    c = pl.program_id(1)

    @pl.when(c == 0)
    def _():
        state_ref[...] = jnp.zeros_like(state_ref)

    heads = range(RET_HEADS)
    v_refs = (v0_ref, v1_ref)
    per = RET_HEADS // len(v_refs)

    def qk(ref, h):
        return ref[:, h * RET_QK_DIM:(h + 1) * RET_QK_DIM]

    def vh(h):
        return v_refs[h // per][:, (h % per) * RET_V_DIM:(h % per + 1) * RET_V_DIM]

    def vsl(h):
        return slice(h * RET_V_DIM, (h + 1) * RET_V_DIM)

    s = [lax.dot_general(qk(q_ref, h), qk(k_ref, h), (((1,), (1,)), ((), ())), preferred_element_type=F32)
         for h in heads]
    cross = [jnp.dot(qk(q_ref, h), state_ref[h].astype(BF16), preferred_element_type=F32) for h in heads]
    kv = []
    for h in heads:
        kz = (qk(k_ref, h).astype(F32) * zeta_ref[h]).astype(BF16)
        kv.append(lax.dot_general(kz, vh(h), (((0,), (0,)), ((), ())), preferred_element_type=F32))
    inner = [jnp.dot((s[h] * decay_ref[h]).astype(BF16), vh(h), preferred_element_type=F32) for h in heads]
    for h in heads:
        state_ref[h] = state_ref[h] * cd_ref[h] + kv[h]
    for h in heads:
        o = inner[h] + cross[h] * xi_ref[h]
        mu = jnp.mean(o, axis=-1, keepdims=True)
        d = o - mu
        var = jnp.mean(d * d, axis=-1, keepdims=True)
        y = d * lax.rsqrt(var + EPS)
        g = g_ref[:, vsl(h)]
        o_ref[:, vsl(h)] = (y * gain_ref[:, vsl(h)] * (g * jax.nn.sigmoid(g))).astype(o_ref.dtype)


def _retention(proj_bf, g_r, gain, tables, *, batch, seq):
    decay, xi, zeta, cd = tables
    ch = decay.shape[-1]
    nc = seq // ch
    t = proj_bf.shape[0]
    q_off = 3 * SB_WIDTH // RET_QK_WIDTH
    v_half = RET_V_WIDTH // 2
    v_off = (3 * SB_WIDTH + 2 * RET_QK_WIDTH) // v_half
    const3 = lambda b, c: (0, 0, 0)
    return pl.pallas_call(
        _ret_kernel,
        grid=(batch, nc),
        in_specs=[
            pl.BlockSpec((ch, RET_QK_WIDTH), lambda b, c: (b * nc + c, q_off)),
            pl.BlockSpec((ch, RET_QK_WIDTH), lambda b, c: (b * nc + c, q_off + 1)),
            pl.BlockSpec((ch, v_half), lambda b, c: (b * nc + c, v_off)),
            pl.BlockSpec((ch, v_half), lambda b, c: (b * nc + c, v_off + 1)),
            pl.BlockSpec((ch, RET_V_WIDTH), lambda b, c: (b * nc + c, 0)),
            pl.BlockSpec((1, RET_V_WIDTH), lambda b, c: (0, 0)),
            pl.BlockSpec((RET_HEADS, ch, ch), const3),
            pl.BlockSpec((RET_HEADS, ch, RET_V_DIM), const3),
            pl.BlockSpec((RET_HEADS, ch, RET_QK_DIM), const3),
            pl.BlockSpec((RET_HEADS, 1, RET_V_DIM), const3),
        ],
        out_specs=pl.BlockSpec((ch, RET_V_WIDTH), lambda b, c: (b * nc + c, 0)),
        out_shape=jax.ShapeDtypeStruct((t, RET_V_WIDTH), BF16),
        scratch_shapes=[pltpu.VMEM((RET_HEADS, RET_QK_DIM, RET_V_DIM), F32)],
        compiler_params=_params(("parallel", "arbitrary")),
        name="retention",
    )(proj_bf, proj_bf, proj_bf, proj_bf, g_r, gain, decay, xi, zeta, cd)


def _retention_tables(ch):
    log_gamma = jnp.log1p(-jnp.exp2(-5.0 - jnp.arange(RET_HEADS, dtype=F32)))
    pos = jnp.arange(ch, dtype=F32)
    lg = log_gamma[:, None, None]
    diff = pos[:, None] - pos[None, :]
    decay = jnp.where(diff >= 0, jnp.exp(jnp.maximum(diff, 0.0) * lg), 0.0)
    zeta = jnp.exp((ch - 1 - pos)[None, :] * log_gamma[:, None])
    xi = jnp.exp((pos + 1)[None, :] * log_gamma[:, None])
    cd = jnp.exp(ch * log_gamma)
    xi_b = jnp.broadcast_to(xi[:, :, None], (RET_HEADS, ch, RET_V_DIM))
    zeta_b = jnp.broadcast_to(zeta[:, :, None], (RET_HEADS, ch, RET_QK_DIM))
    cd_b = jnp.broadcast_to(cd[:, None, None], (RET_HEADS, 1, RET_V_DIM))
    return decay, xi_b, zeta_b, cd_b


def _merge_kernel(x_ref, g_ref, wg_ref, bg_ref, osb_ref, wsb_ref, oret_ref, wret_ref, wo_ref, o_ref):
    x = x_ref[...]
    h = _rms(x, g_ref[...]).astype(BF16)
    gate = jax.nn.sigmoid(jnp.dot(h, wg_ref[...], preferred_element_type=F32) + bg_ref[...])
    y_a = jnp.dot(osb_ref[...], wsb_ref[...], preferred_element_type=F32)
    y_b = jnp.dot(oret_ref[...], wret_ref[...], preferred_element_type=F32)
    merged = (gate[:, :D_MODEL] * y_a + gate[:, D_MODEL:] * y_b).astype(BF16)
    o_ref[...] = x + jnp.dot(merged, wo_ref[...], preferred_element_type=F32)


def _merge(x2, g, wg, bg, o_sb, wsb, o_ret, wret, wo, *, tm):
    t = x2.shape[0]
    const = lambda i: (0, 0)
    return pl.pallas_call(
        _merge_kernel,
        grid=(t // tm,),
        in_specs=[
            pl.BlockSpec((tm, D_MODEL), lambda i: (i, 0)),
            pl.BlockSpec((1, D_MODEL), const),
            pl.BlockSpec((D_MODEL, 2 * D_MODEL), const),
            pl.BlockSpec((1, 2 * D_MODEL), const),
            pl.BlockSpec((tm, SB_WIDTH), lambda i: (i, 0)),
            pl.BlockSpec((SB_WIDTH, D_MODEL), const),
            pl.BlockSpec((tm, RET_V_WIDTH), lambda i: (i, 0)),
            pl.BlockSpec((RET_V_WIDTH, D_MODEL), const),
            pl.BlockSpec((D_MODEL, D_MODEL), const),
        ],
        out_specs=pl.BlockSpec((tm, D_MODEL), lambda i: (i, 0)),
        out_shape=jax.ShapeDtypeStruct((t, D_MODEL), F32),
        compiler_params=_params(("parallel",)),
        name="merge",
    )(x2, g, wg, bg, o_sb, wsb, o_ret, wret, wo)


def _ff_chunks():
    chunks, lo = [], 0
    while lo < D_FF:
        w = min(FF_CHUNK, D_FF - lo)
        chunks.append((lo, w))
        lo += w
    return chunks


def _ffn_kernel(x_ref, xp_ref, g_ref, wu_ref, cw_ref, cb_ref, wd_ref, fg_ref, o_ref, abuf_ref,
                *, tm, tiles_per_seq, final):
    i = pl.program_id(0)
    g = g_ref[...]
    x = x_ref[...]
    h = _rms(x, g).astype(BF16)
    keep = jnp.where(lax.rem(i, tiles_per_seq) == 0, 0.0, 1.0)
    hp = (_rms(xp_ref[...], g) * keep).astype(BF16)
    chunks = _ff_chunks()

    def up(c):
        lo, w = chunks[c]
        wa = wu_ref[:, lo:lo + w]
        a = jnp.dot(h, wa, preferred_element_type=F32)
        b = jnp.dot(h, wu_ref[:, D_FF + lo:D_FF + lo + w], preferred_element_type=F32)
        ap = jnp.dot(hp, wa, preferred_element_type=F32)
        return a, b, ap

    nxt = up(0)
    y = x
    for c, (lo, w) in enumerate(chunks):
        a, b, ap = nxt
        if c + 1 < len(chunks):
            nxt = up(c + 1)
        buf = abuf_ref.at[c % 2]
        buf[0:HALO, 0:w] = ap
        buf[HALO:, 0:w] = a
        cw = cw_ref[:, lo:lo + w]
        conv = (buf[HALO - 2:HALO - 2 + tm, 0:w] * cw[0:1, :]
                + buf[HALO - 1:HALO - 1 + tm, 0:w] * cw[1:2, :]
                + a * cw[2:3, :] + cb_ref[:, lo:lo + w])
        act = (conv * jax.nn.sigmoid(conv) * b).astype(BF16)
        y = y + jnp.dot(act, wd_ref[lo:lo + w, :], preferred_element_type=F32)
    if final:
        y = _rms(y, fg_ref[...])
    o_ref[...] = y


def _conv_ffn(x2, g, w_up, cw, cb, wd, fg, *, tm, seq, final):
    t = x2.shape[0]
    hb = tm // HALO
    kern = functools.partial(_ffn_kernel, tm=tm, tiles_per_seq=seq // tm, final=final)
    const = lambda i: (0, 0)
    resident = dict(pipeline_mode=pl.Buffered(1))
    return pl.pallas_call(
        kern,
        grid=(t // tm,),
        in_specs=[
            pl.BlockSpec((tm, D_MODEL), lambda i: (i, 0)),
            pl.BlockSpec((HALO, D_MODEL), lambda i: (jnp.maximum(i * hb - 1, 0), 0)),
            pl.BlockSpec((1, D_MODEL), const),
            pl.BlockSpec((D_MODEL, 2 * D_FF), const, **resident),
            pl.BlockSpec((CONV_WIDTH, D_FF), const),
            pl.BlockSpec((1, D_FF), const),
            pl.BlockSpec((D_FF, D_MODEL), const, **resident),
            pl.BlockSpec((1, D_MODEL), const),
        ],
        out_specs=pl.BlockSpec((tm, D_MODEL), lambda i: (i, 0)),
        out_shape=jax.ShapeDtypeStruct((t, D_MODEL), F32),
        scratch_shapes=[pltpu.VMEM((2, tm + HALO, FF_CHUNK), F32)],
        compiler_params=_params(("parallel",)),
        name="conv_ffn",
    )(x2, x2, g, w_up, cw, cb, wd, fg)


def kernel(x, attn_norm, w_in, b_gate, sb_w_out, ret_norm, ret_w_out, w_o, ffn_norm, w_up, conv_w, conv_b,
           w_down, final_norm):
    batch, seq, d = x.shape
    assert d == D_MODEL
    depth = w_in.shape[0]
    t = batch * seq
    tm = min(512, seq)
    assert seq % tm == 0 and tm % HALO == 0

    half = RET_QK_DIM // 2
    inv = ROPE_BASE ** (-jnp.arange(half, dtype=F32) / half)
    ang = jnp.arange(seq, dtype=F32)[:, None] * inv[None, :]
    cosf = jnp.concatenate([jnp.cos(ang), jnp.cos(ang)], axis=-1)
    sins = jnp.concatenate([-jnp.sin(ang), jnp.sin(ang)], axis=-1)

    ch = min(RET_CHUNK, seq)
    tables = _retention_tables(ch)
    blk = min(SB_BLOCK, seq)
    idx = jnp.arange(blk)
    tri = (idx[:, None] > idx[None, :]).astype(BF16)
    tri = jnp.concatenate([tri, tri], axis=0)

    x2 = x.reshape(t, d)
    for l in range(depth):
        w_in_bf = w_in[l].astype(BF16)
        g_attn = attn_norm[l][None, :]
        proj_bf, g_r = _in_proj(x2, g_attn, w_in_bf[:, :PROJ_COLS], cosf, sins, tm=tm, seq=seq)
        o_sb = _sb_attention(proj_bf, tri, batch=batch, seq=seq)
        o_ret = _retention(proj_bf, g_r, ret_norm[l][None, :], tables, batch=batch, seq=seq)
        x2 = _merge(x2, g_attn, w_in_bf[:, PROJ_COLS:], b_gate[l][None, :], o_sb, sb_w_out[l].astype(BF16),
                    o_ret, ret_w_out[l].astype(BF16), w_o[l].astype(BF16), tm=tm)
        x2 = _conv_ffn(x2, ffn_norm[l][None, :], w_up[l].astype(BF16), conv_w[l], conv_b[l][None, :],
                       w_down[l].astype(BF16), final_norm[None, :], tm=tm, seq=seq,
                       final=(l == depth - 1))
    return x2.reshape(batch, seq, d)
```

```python
import functools

import jax
import jax.numpy as jnp
import numpy as np
from jax import lax
from jax.experimental import pallas as pl
from jax.experimental.pallas import tpu as pltpu

F32 = jnp.float32
BF16 = jnp.bfloat16

LANES = 128
D_MODEL = 1024
SB_HEAD_DIM = 64
SB_WIDTH = 512
RET_HEADS = 4
RET_QK_DIM = 128
RET_V_DIM = 256
RET_QK_WIDTH = RET_HEADS * RET_QK_DIM
RET_V_WIDTH = RET_HEADS * RET_V_DIM
ROPE_BASE = 10000.0
D_FF = 2816
CONV_WIDTH = 3
EPS = 1e-6

PROJ_BF_COLS = 3 * SB_WIDTH + 2 * RET_QK_WIDTH + RET_V_WIDTH
PROJ_COLS = PROJ_BF_COLS + RET_V_WIDTH
PROJ_CHUNK = 512

RET_CHUNK = 256
SB_BLOCK = 256
SB_CHAINS = 2
SB_DEPTH = 2
SB_SKIP = 104.0
SB_MASKED = 1e30
HALO = 16
FF_CHUNK = 512

VMEM_LIMIT = 48 * 1024 * 1024


def _rms(x, g):
    ms = jnp.mean(x * x, axis=-1, keepdims=True)
    return x * lax.rsqrt(ms + EPS) * g


def _params(sem):
    return pltpu.CompilerParams(dimension_semantics=sem, vmem_limit_bytes=VMEM_LIMIT)


def _inproj_kernel(x_ref, g_ref, w_ref, cos_ref, sin_ref, obf_ref, og_ref):
    h = _rms(x_ref[...], g_ref[...]).astype(BF16)
    n_bf = PROJ_BF_COLS // PROJ_CHUNK
    for c in range(PROJ_COLS // PROJ_CHUNK):
        lo = c * PROJ_CHUNK
        acc = jnp.dot(h, w_ref[:, lo:lo + PROJ_CHUNK], preferred_element_type=F32)
        if c == 0:
            acc = acc * (SB_HEAD_DIM ** -0.5)
        if c in (3, 4):
            cos = cos_ref[...]
            sin = sin_ref[...]
            for hd in range(PROJ_CHUNK // LANES):
                t = acc[:, hd * LANES:(hd + 1) * LANES]
                o = t * cos + pltpu.roll(t, LANES // 2, axis=1) * sin
                if c == 4:
                    o = o * (RET_QK_DIM ** -0.5)
                obf_ref[:, lo + hd * LANES:lo + (hd + 1) * LANES] = o.astype(BF16)
        elif c < n_bf:
            obf_ref[:, lo:lo + PROJ_CHUNK] = acc.astype(BF16)
        else:
            og_ref[:, lo - PROJ_BF_COLS:lo - PROJ_BF_COLS + PROJ_CHUNK] = acc


def _in_proj(x2, g, w, cosf, sins, *, tm, seq):
    t = x2.shape[0]
    spt = seq // tm
    return pl.pallas_call(
        _inproj_kernel,
        grid=(t // tm,),
        in_specs=[
            pl.BlockSpec((tm, D_MODEL), lambda i: (i, 0)),
            pl.BlockSpec((1, D_MODEL), lambda i: (0, 0)),
            pl.BlockSpec((D_MODEL, PROJ_COLS), lambda i: (0, 0)),
            pl.BlockSpec((tm, LANES), lambda i: (i % spt, 0)),
            pl.BlockSpec((tm, LANES), lambda i: (i % spt, 0)),
        ],
        out_specs=[
            pl.BlockSpec((tm, PROJ_BF_COLS), lambda i: (i, 0)),
            pl.BlockSpec((tm, RET_V_WIDTH), lambda i: (i, 0)),
        ],
        out_shape=[
            jax.ShapeDtypeStruct((t, PROJ_BF_COLS), BF16),
            jax.ShapeDtypeStruct((t, RET_V_WIDTH), F32),
        ],
        compiler_params=_params(("parallel",)),
        name="in_proj",
    )(x2, g, w, cosf, sins)


def _sb_kernel(q_ref, k_ref, v_ref, tri_ref, o_ref, qq_ref, acc_ref, r_ref, *, blk, n_chain, n_pair):
    i = pl.program_id(1)
    lane = lax.broadcasted_iota(jnp.int32, (blk, LANES), 1)
    first = lane < SB_HEAD_DIM
    for p in range(n_pair):
        q = q_ref[:, p * LANES:(p + 1) * LANES]
        zero = jnp.zeros_like(q)
        qq_ref[p, 0:blk, :] = jnp.where(first, q, zero)
        qq_ref[p, blk:, :] = jnp.where(first, zero, q)
    rows = 2 * blk // n_chain
    sls = [pl.ds(c * rows, rows) for c in range(n_chain)]

    def sweep(blocks, fresh, pairs):
        tri2 = tri_ref[...]
        starts = [pl.multiple_of(j * blk, blk) for j, _ in blocks]
        units = [(b, p, c) for b in range(len(blocks)) for p in pairs for c in range(n_chain)]
        n = len(units)
        zs, st = {}, {}
        r = {(p, c): None if fresh else r_ref[p, sls[c], :] for p in pairs for c in range(n_chain)}
        o = {(p, c): None for p in pairs for c in range(n_chain)}

        def kv(ref, b, p):
            return ref[pl.ds(starts[b], blk), p * LANES:(p + 1) * LANES]

        def scores(u):
            b, p, c = units[u]
            z = lax.dot_general(qq_ref[p, sls[c], :], kv(k_ref, b, p), (((1,), (1,)), ((), ())),
                                preferred_element_type=F32)
            if blocks[b][1]:
                row = (lax.broadcasted_iota(jnp.int32, z.shape, 0) + c * rows) & (blk - 1)
                col = lax.broadcasted_iota(jnp.int32, z.shape, 1)
                z = jnp.where(col < row, z, -SB_MASKED)
            zs[u] = z

        def softplus_cumsum(u):
            z = zs.pop(u)
            neg_abs = pltpu.bitcast(pltpu.bitcast(z, jnp.uint32) | jnp.uint32(0x80000000), F32)
            sp = jnp.maximum(z, 0.0) + jnp.log(1.0 + jnp.exp(neg_abs))
            hi = sp.astype(BF16)
            lo = (sp - hi.astype(F32)).astype(BF16)
            cs = jnp.dot(jnp.concatenate([hi, lo], axis=1), tri2,
                         preferred_element_type=F32)
            st[u] = (z, cs, jnp.sum(sp, axis=-1, keepdims=True))

        def weights_values(u):
            b, p, c = units[u]
            z, cs, rs = st.pop(u)
            e = z - cs
            rc = r[p, c]
            if rc is not None:
                e = e - (rc if rc.shape[1] == 1 else jnp.concatenate([rc] * (blk // LANES), axis=1))
            part = jnp.dot(jnp.exp(e).astype(BF16), kv(v_ref, b, p), preferred_element_type=F32)
            o[p, c] = part if o[p, c] is None else o[p, c] + part
            r[p, c] = rs if rc is None else rc + rs

        for u in range(min(SB_DEPTH, n)):
            scores(u)
        for u in range(n):
            softplus_cumsum(u)
            if u + SB_DEPTH < n:
                scores(u + SB_DEPTH)
            if u >= 1:
                weights_values(u - 1)
        weights_values(n - 1)
        for p in pairs:
            for c in range(n_chain):
                acc_ref[p, sls[c], :] = o[p, c] if fresh else acc_ref[p, sls[c], :] + o[p, c]
                r_ref[p, sls[c], :] = jnp.broadcast_to(r[p, c], (rows, LANES))

    all_pairs = list(range(n_pair))

    @pl.when(i == 0)
    def _():
        sweep([(i, True)], True, all_pairs)

    @pl.when(i > 0)
    def _():
        sweep([(i, True), (i - 1, False)], True, all_pairs)

    def cond(carry):
        j, rmin = carry
        return jnp.logical_and(j >= 0, rmin < SB_SKIP)

    rmins = [jnp.min(r_ref[p]) for p in all_pairs]
    for p in all_pairs:
        def body(carry, p=p):
            j, _ = carry
            sweep([(j, False)], False, [p])
            return j - 1, jnp.min(r_ref[p])

        lax.while_loop(cond, body, (i - 2, rmins[p]))
    for p in all_pairs:
        acc = acc_ref[p]
        o_ref[:, p * LANES:(p + 1) * LANES] = jnp.where(first, acc[:blk], acc[blk:]).astype(o_ref.dtype)


def _sb_attention(proj_bf, tri, *, batch, seq):
    blk = min(SB_BLOCK, seq)
    nq = seq // blk
    npair = SB_WIDTH // LANES
    t = proj_bf.shape[0]
    return pl.pallas_call(
        functools.partial(_sb_kernel, blk=blk, n_chain=SB_CHAINS, n_pair=npair),
        grid=(batch, nq),
        in_specs=[
            pl.BlockSpec((blk, SB_WIDTH), lambda b, i: (b * nq + i, 0)),
            pl.BlockSpec((seq, SB_WIDTH), lambda b, i: (b, 1)),
            pl.BlockSpec((seq, SB_WIDTH), lambda b, i: (b, 2)),
            pl.BlockSpec((2 * blk, blk), lambda b, i: (0, 0)),
        ],
        out_specs=pl.BlockSpec((blk, SB_WIDTH), lambda b, i: (b * nq + i, 0)),
        out_shape=jax.ShapeDtypeStruct((t, SB_WIDTH), BF16),
        scratch_shapes=[pltpu.VMEM((npair, 2 * blk, LANES), BF16), pltpu.VMEM((npair, 2 * blk, LANES), F32),
                        pltpu.VMEM((npair, 2 * blk, LANES), F32)],
        compiler_params=_params(("parallel", "arbitrary")),
        name="sb_attn",
    )(proj_bf, proj_bf, proj_bf, tri)


def _ret_kernel(q_ref, k_ref, v0_ref, v1_ref, g_ref, gain_ref, decay_ref, xi_ref, zeta_ref, cd_ref, o_ref,
                state_ref):
    c = pl.program_id(1)

    @pl.when(c == 0)
    def _():
        state_ref[...] = jnp.zeros_like(state_ref)

    heads = range(RET_HEADS)
    v_refs = (v0_ref, v1_ref)
    per = RET_HEADS // len(v_refs)

    def qk(ref, h):
        return ref[:, h * RET_QK_DIM:(h + 1) * RET_QK_DIM]

    def vh(h):
        return v_refs[h // per][:, (h % per) * RET_V_DIM:(h % per + 1) * RET_V_DIM]

    def vsl(h):
        return slice(h * RET_V_DIM, (h + 1) * RET_V_DIM)

    s = [lax.dot_general(qk(q_ref, h), qk(k_ref, h), (((1,), (1,)), ((), ())), preferred_element_type=F32)
         for h in heads]
    cross = [jnp.dot(qk(q_ref, h), state_ref[h].astype(BF16), preferred_element_type=F32) for h in heads]
    kv = []
    for h in heads:
        kz = (qk(k_ref, h).astype(F32) * zeta_ref[h]).astype(BF16)
        kv.append(lax.dot_general(kz, vh(h), (((0,), (0,)), ((), ())), preferred_element_type=F32))
    inner = [jnp.dot((s[h] * decay_ref[h]).astype(BF16), vh(h), preferred_element_type=F32) for h in heads]
    for h in heads:
        state_ref[h] = state_ref[h] * cd_ref[h] + kv[h]
    for h in heads:
        o = inner[h] + cross[h] * xi_ref[h]
        mu = jnp.mean(o, axis=-1, keepdims=True)
        d = o - mu
        var = jnp.mean(d * d, axis=-1, keepdims=True)
        y = d * lax.rsqrt(var + EPS)
        g = g_ref[:, vsl(h)]
        o_ref[:, vsl(h)] = (y * gain_ref[:, vsl(h)] * (g * jax.nn.sigmoid(g))).astype(o_ref.dtype)


def _retention(proj_bf, g_r, gain, tables, *, batch, seq):
    decay, xi, zeta, cd = tables
    ch = decay.shape[-1]
    nc = seq // ch
    t = proj_bf.shape[0]
    q_off = 3 * SB_WIDTH // RET_QK_WIDTH
    v_half = RET_V_WIDTH // 2
    v_off = (3 * SB_WIDTH + 2 * RET_QK_WIDTH) // v_half
    const3 = lambda b, c: (0, 0, 0)
    return pl.pallas_call(
        _ret_kernel,
        grid=(batch, nc),
        in_specs=[
            pl.BlockSpec((ch, RET_QK_WIDTH), lambda b, c: (b * nc + c, q_off)),
            pl.BlockSpec((ch, RET_QK_WIDTH), lambda b, c: (b * nc + c, q_off + 1)),
            pl.BlockSpec((ch, v_half), lambda b, c: (b * nc + c, v_off)),
            pl.BlockSpec((ch, v_half), lambda b, c: (b * nc + c, v_off + 1)),
            pl.BlockSpec((ch, RET_V_WIDTH), lambda b, c: (b * nc + c, 0)),
            pl.BlockSpec((1, RET_V_WIDTH), lambda b, c: (0, 0)),
            pl.BlockSpec((RET_HEADS, ch, ch), const3),
            pl.BlockSpec((RET_HEADS, ch, RET_V_DIM), const3),
            pl.BlockSpec((RET_HEADS, ch, RET_QK_DIM), const3),
            pl.BlockSpec((RET_HEADS, 1, RET_V_DIM), const3),
        ],
        out_specs=pl.BlockSpec((ch, RET_V_WIDTH), lambda b, c: (b * nc + c, 0)),
        out_shape=jax.ShapeDtypeStruct((t, RET_V_WIDTH), BF16),
        scratch_shapes=[pltpu.VMEM((RET_HEADS, RET_QK_DIM, RET_V_DIM), F32)],
        compiler_params=_params(("parallel", "arbitrary")),
        name="retention",
    )(proj_bf, proj_bf, proj_bf, proj_bf, g_r, gain, decay, xi, zeta, cd)


def _retention_tables(ch):
    log_gamma = jnp.log1p(-jnp.exp2(-5.0 - jnp.arange(RET_HEADS, dtype=F32)))
    pos = jnp.arange(ch, dtype=F32)
    lg = log_gamma[:, None, None]
    diff = pos[:, None] - pos[None, :]
    decay = jnp.where(diff >= 0, jnp.exp(jnp.maximum(diff, 0.0) * lg), 0.0)
    zeta = jnp.exp((ch - 1 - pos)[None, :] * log_gamma[:, None])
    xi = jnp.exp((pos + 1)[None, :] * log_gamma[:, None])
    cd = jnp.exp(ch * log_gamma)
    xi_b = jnp.broadcast_to(xi[:, :, None], (RET_HEADS, ch, RET_V_DIM))
    zeta_b = jnp.broadcast_to(zeta[:, :, None], (RET_HEADS, ch, RET_QK_DIM))
    cd_b = jnp.broadcast_to(cd[:, None, None], (RET_HEADS, 1, RET_V_DIM))
    return decay, xi_b, zeta_b, cd_b


def _merge_kernel(x_ref, g_ref, wg_ref, bg_ref, osb_ref, wsb_ref, oret_ref, wret_ref, wo_ref, o_ref):
    x = x_ref[...]
    h = _rms(x, g_ref[...]).astype(BF16)
    gate = jax.nn.sigmoid(jnp.dot(h, wg_ref[...], preferred_element_type=F32) + bg_ref[...])
    y_a = jnp.dot(osb_ref[...], wsb_ref[...], preferred_element_type=F32)
    y_b = jnp.dot(oret_ref[...], wret_ref[...], preferred_element_type=F32)
    merged = (gate[:, :D_MODEL] * y_a + gate[:, D_MODEL:] * y_b).astype(BF16)
    o_ref[...] = x + jnp.dot(merged, wo_ref[...], preferred_element_type=F32)


def _merge(x2, g, wg, bg, o_sb, wsb, o_ret, wret, wo, *, tm):
    t = x2.shape[0]
    const = lambda i: (0, 0)
    return pl.pallas_call(
        _merge_kernel,
        grid=(t // tm,),
        in_specs=[
            pl.BlockSpec((tm, D_MODEL), lambda i: (i, 0)),
            pl.BlockSpec((1, D_MODEL), const),
            pl.BlockSpec((D_MODEL, 2 * D_MODEL), const),
            pl.BlockSpec((1, 2 * D_MODEL), const),
            pl.BlockSpec((tm, SB_WIDTH), lambda i: (i, 0)),
            pl.BlockSpec((SB_WIDTH, D_MODEL), const),
            pl.BlockSpec((tm, RET_V_WIDTH), lambda i: (i, 0)),
            pl.BlockSpec((RET_V_WIDTH, D_MODEL), const),
            pl.BlockSpec((D_MODEL, D_MODEL), const),
        ],
        out_specs=pl.BlockSpec((tm, D_MODEL), lambda i: (i, 0)),
        out_shape=jax.ShapeDtypeStruct((t, D_MODEL), F32),
        compiler_params=_params(("parallel",)),
        name="merge",
    )(x2, g, wg, bg, o_sb, wsb, o_ret, wret, wo)


def _ff_chunks():
    chunks, lo = [], 0
    while lo < D_FF:
        w = min(FF_CHUNK, D_FF - lo)
        chunks.append((lo, w))
        lo += w
    return chunks


def _ffn_kernel(x_ref, xp_ref, g_ref, wu_ref, cw_ref, cb_ref, wd_ref, fg_ref, o_ref, abuf_ref,
                *, tm, tiles_per_seq, final):
    i = pl.program_id(0)
    g = g_ref[...]
    x = x_ref[...]
    h = _rms(x, g).astype(BF16)
    keep = jnp.where(lax.rem(i, tiles_per_seq) == 0, 0.0, 1.0)
    hp = (_rms(xp_ref[...], g) * keep).astype(BF16)
    chunks = _ff_chunks()

    def up(c):
        lo, w = chunks[c]
        wa = wu_ref[:, lo:lo + w]
        a = jnp.dot(h, wa, preferred_element_type=F32)
        b = jnp.dot(h, wu_ref[:, D_FF + lo:D_FF + lo + w], preferred_element_type=F32)
        ap = jnp.dot(hp, wa, preferred_element_type=F32)
        return a, b, ap

    nxt = up(0)
    y = x
    for c, (lo, w) in enumerate(chunks):
        a, b, ap = nxt
        if c + 1 < len(chunks):
            nxt = up(c + 1)
        buf = abuf_ref.at[c % 2]
        buf[0:HALO, 0:w] = ap
        buf[HALO:, 0:w] = a
        cw = cw_ref[:, lo:lo + w]
        conv = (buf[HALO - 2:HALO - 2 + tm, 0:w] * cw[0:1, :]
                + buf[HALO - 1:HALO - 1 + tm, 0:w] * cw[1:2, :]
                + a * cw[2:3, :] + cb_ref[:, lo:lo + w])
        act = (conv * jax.nn.sigmoid(conv) * b).astype(BF16)
        y = y + jnp.dot(act, wd_ref[lo:lo + w, :], preferred_element_type=F32)
    if final:
        y = _rms(y, fg_ref[...])
    o_ref[...] = y


def _conv_ffn(x2, g, w_up, cw, cb, wd, fg, *, tm, seq, final):
    t = x2.shape[0]
    hb = tm // HALO
    kern = functools.partial(_ffn_kernel, tm=tm, tiles_per_seq=seq // tm, final=final)
    const = lambda i: (0, 0)
    resident = dict(pipeline_mode=pl.Buffered(1))
    return pl.pallas_call(
        kern,
        grid=(t // tm,),
        in_specs=[
            pl.BlockSpec((tm, D_MODEL), lambda i: (i, 0)),
            pl.BlockSpec((HALO, D_MODEL), lambda i: (jnp.maximum(i * hb - 1, 0), 0)),
            pl.BlockSpec((1, D_MODEL), const),
            pl.BlockSpec((D_MODEL, 2 * D_FF), const, **resident),
            pl.BlockSpec((CONV_WIDTH, D_FF), const),
            pl.BlockSpec((1, D_FF), const),
            pl.BlockSpec((D_FF, D_MODEL), const, **resident),
            pl.BlockSpec((1, D_MODEL), const),
        ],
        out_specs=pl.BlockSpec((tm, D_MODEL), lambda i: (i, 0)),
        out_shape=jax.ShapeDtypeStruct((t, D_MODEL), F32),
        scratch_shapes=[pltpu.VMEM((2, tm + HALO, FF_CHUNK), F32)],
        compiler_params=_params(("parallel",)),
        name="conv_ffn",
    )(x2, x2, g, w_up, cw, cb, wd, fg)


def kernel(x, attn_norm, w_in, b_gate, sb_w_out, ret_norm, ret_w_out, w_o, ffn_norm, w_up, conv_w, conv_b,
           w_down, final_norm):
    batch, seq, d = x.shape
    assert d == D_MODEL
    depth = w_in.shape[0]
    t = batch * seq
    tm = min(512, seq)
    assert seq % tm == 0 and tm % HALO == 0

    half = RET_QK_DIM // 2
    inv = ROPE_BASE ** (-jnp.arange(half, dtype=F32) / half)
    ang = jnp.arange(seq, dtype=F32)[:, None] * inv[None, :]
    cosf = jnp.concatenate([jnp.cos(ang), jnp.cos(ang)], axis=-1)
    sins = jnp.concatenate([-jnp.sin(ang), jnp.sin(ang)], axis=-1)

    ch = min(RET_CHUNK, seq)
    tables = _retention_tables(ch)
    blk = min(SB_BLOCK, seq)
    idx = jnp.arange(blk)
    tri = (idx[:, None] >= idx[None, :]).astype(BF16)
    tri = jnp.concatenate([tri, tri], axis=0)

    x2 = x.reshape(t, d)
    for l in range(depth):
        w_in_bf = w_in[l].astype(BF16)
        g_attn = attn_norm[l][None, :]
        proj_bf, g_r = _in_proj(x2, g_attn, w_in_bf[:, :PROJ_COLS], cosf, sins, tm=tm, seq=seq)
        o_sb = _sb_attention(proj_bf, tri, batch=batch, seq=seq)
        o_ret = _retention(proj_bf, g_r, ret_norm[l][None, :], tables, batch=batch, seq=seq)
        x2 = _merge(x2, g_attn, w_in_bf[:, PROJ_COLS:], b_gate[l][None, :], o_sb, sb_w_out[l].astype(BF16),
                    o_ret, ret_w_out[l].astype(BF16), w_o[l].astype(BF16), tm=tm)
        x2 = _conv_ffn(x2, ffn_norm[l][None, :], w_up[l].astype(BF16), conv_w[l], conv_b[l][None, :],
                       w_down[l].astype(BF16), final_norm[None, :], tm=tm, seq=seq,
                       final=(l == depth - 1))
    return x2.reshape(batch, seq, d)
```

```python
import functools

import jax
import jax.numpy as jnp
import numpy as np
from jax import lax
from jax.experimental import pallas as pl
from jax.experimental.pallas import tpu as pltpu

F32 = jnp.float32
BF16 = jnp.bfloat16

LANES = 128
D_MODEL = 1024
SB_HEAD_DIM = 64
SB_WIDTH = 512
RET_HEADS = 4
RET_QK_DIM = 128
RET_V_DIM = 256
RET_QK_WIDTH = RET_HEADS * RET_QK_DIM
RET_V_WIDTH = RET_HEADS * RET_V_DIM
ROPE_BASE = 10000.0
D_FF = 2816
CONV_WIDTH = 3
EPS = 1e-6

PROJ_BF_COLS = 3 * SB_WIDTH + 2 * RET_QK_WIDTH + RET_V_WIDTH
PROJ_COLS = PROJ_BF_COLS + RET_V_WIDTH
PROJ_CHUNK = 512

RET_CHUNK = 256
SB_BLOCK = 256
SB_CHAINS = 2
SB_DEPTH = 2
SB_SKIP = 104.0
SB_MASKED = 1e30
HALO = 16
FF_CHUNK = 1024

VMEM_LIMIT = 48 * 1024 * 1024


def _rms(x, g):
    ms = jnp.mean(x * x, axis=-1, keepdims=True)
    return x * lax.rsqrt(ms + EPS) * g


def _params(sem):
    return pltpu.CompilerParams(dimension_semantics=sem, vmem_limit_bytes=VMEM_LIMIT)


def _inproj_kernel(x_ref, g_ref, w_ref, cos_ref, sin_ref, gain_ref, decay_ref, xi_ref, zeta_ref, cd_ref,
                   osb_ref, oret_ref, state_ref, *, tm, ch, tiles_per_seq):
    i = pl.program_id(0)

    @pl.when(lax.rem(i, tiles_per_seq) == 0)
    def _():
        state_ref[...] = jnp.zeros_like(state_ref)

    h = _rms(x_ref[...], g_ref[...]).astype(BF16)
    heads = range(RET_HEADS)
    chunks = range(tm // ch)

    def proj(lo, width):
        return jnp.dot(h, w_ref[:, lo:lo + width], preferred_element_type=F32)

    def rotary(lo, scale):
        acc = proj(lo, RET_QK_WIDTH)
        cos = cos_ref[...]
        sin = sin_ref[...]
        out = []
        for hd in heads:
            t = acc[:, hd * RET_QK_DIM:(hd + 1) * RET_QK_DIM]
            o = t * cos + pltpu.roll(t, RET_QK_DIM // 2, axis=1) * sin
            out.append((o if scale is None else o * scale).astype(BF16))
        return out

    def sb_chunk(c):
        lo = c * PROJ_CHUNK
        acc = proj(lo, PROJ_CHUNK)
        if c == 0:
            acc = acc * (SB_HEAD_DIM ** -0.5)
        osb_ref[:, lo:lo + PROJ_CHUNK] = acc.astype(BF16)

    def rows(a, c):
        return a[c * ch:(c + 1) * ch]

    def vsl(hd):
        return slice(hd * RET_V_DIM, (hd + 1) * RET_V_DIM)

    q_lo = 3 * SB_WIDTH
    q = rotary(q_lo, None)
    k = rotary(q_lo + RET_QK_WIDTH, RET_QK_DIM ** -0.5)
    v_lo = q_lo + 2 * RET_QK_WIDTH
    v = proj(v_lo, RET_V_WIDTH).astype(BF16)
    s = {(c, hd): lax.dot_general(rows(q[hd], c), rows(k[hd], c), (((1,), (1,)), ((), ())),
                                  preferred_element_type=F32) for c in chunks for hd in heads}
    gate = proj(v_lo + RET_V_WIDTH, RET_V_WIDTH)
    sb_order = list(range(3 * SB_WIDTH // PROJ_CHUNK))
    for c in chunks:
        cross = [jnp.dot(rows(q[hd], c), state_ref[hd].astype(BF16), preferred_element_type=F32) for hd in heads]
        kv = []
        for hd in heads:
            kz = (rows(k[hd], c).astype(F32) * zeta_ref[hd]).astype(BF16)
            kv.append(lax.dot_general(kz, rows(v, c)[:, vsl(hd)], (((0,), (0,)), ((), ())),
                                      preferred_element_type=F32))
        inner = [jnp.dot((s[c, hd] * decay_ref[hd]).astype(BF16), rows(v, c)[:, vsl(hd)],
                         preferred_element_type=F32) for hd in heads]
        for hd in heads:
            state_ref[hd] = state_ref[hd] * cd_ref[hd] + kv[hd]
        if sb_order:
            sb_chunk(sb_order.pop(0))
        for hd in heads:
            o = inner[hd] + cross[hd] * xi_ref[hd]
            mu = jnp.mean(o, axis=-1, keepdims=True)
            d = o - mu
            var = jnp.mean(d * d, axis=-1, keepdims=True)
            y = d * lax.rsqrt(var + EPS)
            gt = rows(gate, c)[:, vsl(hd)]
            oret_ref[c * ch:(c + 1) * ch, vsl(hd)] = (
                y * gain_ref[:, vsl(hd)] * (gt * jax.nn.sigmoid(gt))).astype(oret_ref.dtype)
        if sb_order:
            sb_chunk(sb_order.pop(0))
    while sb_order:
        sb_chunk(sb_order.pop(0))


def _in_proj(x2, g, w, cosf, sins, gain, tables, *, tm, seq):
    decay, xi, zeta, cd = tables
    ch = decay.shape[-1]
    t = x2.shape[0]
    spt = seq // tm
    const = lambda i: (0, 0)
    const3 = lambda i: (0, 0, 0)
    resident = dict(pipeline_mode=pl.Buffered(1))
    kern = functools.partial(_inproj_kernel, tm=tm, ch=ch, tiles_per_seq=spt)
    return pl.pallas_call(
        kern,
        grid=(t // tm,),
        in_specs=[
            pl.BlockSpec((tm, D_MODEL), lambda i: (i, 0)),
            pl.BlockSpec((1, D_MODEL), const),
            pl.BlockSpec((D_MODEL, PROJ_COLS), const, **resident),
            pl.BlockSpec((tm, LANES), lambda i: (i % spt, 0)),
            pl.BlockSpec((tm, LANES), lambda i: (i % spt, 0)),
            pl.BlockSpec((1, RET_V_WIDTH), const),
            pl.BlockSpec((RET_HEADS, ch, ch), const3, **resident),
            pl.BlockSpec((RET_HEADS, ch, RET_V_DIM), const3, **resident),
            pl.BlockSpec((RET_HEADS, ch, RET_QK_DIM), const3, **resident),
            pl.BlockSpec((RET_HEADS, 1, RET_V_DIM), const3),
        ],
        out_specs=[
            pl.BlockSpec((tm, 3 * SB_WIDTH), lambda i: (i, 0)),
            pl.BlockSpec((tm, RET_V_WIDTH), lambda i: (i, 0)),
        ],
        out_shape=[
            jax.ShapeDtypeStruct((t, 3 * SB_WIDTH), BF16),
            jax.ShapeDtypeStruct((t, RET_V_WIDTH), BF16),
        ],
        scratch_shapes=[pltpu.VMEM((RET_HEADS, RET_QK_DIM, RET_V_DIM), F32)],
        compiler_params=_params(("arbitrary",)),
        name="in_proj",
    )(x2, g, w, cosf, sins, gain, decay, xi, zeta, cd)


def _sb_kernel(q_ref, k_ref, v_ref, tri_ref, o_ref, qq_ref, acc_ref, r_ref, *, blk, n_chain, n_pair):
    i = pl.program_id(1)
    lane = lax.broadcasted_iota(jnp.int32, (blk, LANES), 1)
    first = lane < SB_HEAD_DIM
    for p in range(n_pair):
        q = q_ref[:, p * LANES:(p + 1) * LANES]
        zero = jnp.zeros_like(q)
        qq_ref[p, 0:blk, :] = jnp.where(first, q, zero)
        qq_ref[p, blk:, :] = jnp.where(first, zero, q)
    rows = 2 * blk // n_chain
    sls = [pl.ds(c * rows, rows) for c in range(n_chain)]

    def sweep(blocks, fresh, pairs):
        tri2 = tri_ref[...]
        starts = [pl.multiple_of(j * blk, blk) for j, _ in blocks]
        units = [(b, p, c) for b in range(len(blocks)) for p in pairs for c in range(n_chain)]
        n = len(units)
        zs, st = {}, {}
        r = {(p, c): None if fresh else r_ref[p, sls[c], :] for p in pairs for c in range(n_chain)}
        o = {(p, c): None for p in pairs for c in range(n_chain)}

        def kv(ref, b, p):
            return ref[pl.ds(starts[b], blk), p * LANES:(p + 1) * LANES]

        def scores(u):
            b, p, c = units[u]
            z = lax.dot_general(qq_ref[p, sls[c], :], kv(k_ref, b, p), (((1,), (1,)), ((), ())),
                                preferred_element_type=F32)
            if blocks[b][1]:
                row = (lax.broadcasted_iota(jnp.int32, z.shape, 0) + c * rows) & (blk - 1)
                col = lax.broadcasted_iota(jnp.int32, z.shape, 1)
                z = jnp.where(col < row, z, -SB_MASKED)
            zs[u] = z

        def softplus_cumsum(u):
            z = zs.pop(u)
            neg_abs = pltpu.bitcast(pltpu.bitcast(z, jnp.uint32) | jnp.uint32(0x80000000), F32)
            sp = jnp.maximum(z, 0.0) + jnp.log(1.0 + jnp.exp(neg_abs))
            hi = sp.astype(BF16)
            lo = (sp - hi.astype(F32)).astype(BF16)
            cs = jnp.dot(jnp.concatenate([hi, lo], axis=1), tri2,
                         preferred_element_type=F32)
            st[u] = (z, cs, jnp.sum(sp, axis=-1, keepdims=True))

        def weights_values(u):
            b, p, c = units[u]
            z, cs, rs = st.pop(u)
            e = z - cs
            rc = r[p, c]
            if rc is not None:
                e = e - (rc if rc.shape[1] == 1 else jnp.concatenate([rc] * (blk // LANES), axis=1))
            part = jnp.dot(jnp.exp(e).astype(BF16), kv(v_ref, b, p), preferred_element_type=F32)
            o[p, c] = part if o[p, c] is None else o[p, c] + part
            r[p, c] = rs if rc is None else rc + rs

        for u in range(min(SB_DEPTH, n)):
            scores(u)
        for u in range(n):
            softplus_cumsum(u)
            if u + SB_DEPTH < n:
                scores(u + SB_DEPTH)
            if u >= 1:
                weights_values(u - 1)
        weights_values(n - 1)
        for p in pairs:
            for c in range(n_chain):
                acc_ref[p, sls[c], :] = o[p, c] if fresh else acc_ref[p, sls[c], :] + o[p, c]
                r_ref[p, sls[c], :] = jnp.broadcast_to(r[p, c], (rows, LANES))

    all_pairs = list(range(n_pair))

    @pl.when(i == 0)
    def _():
        sweep([(i, True)], True, all_pairs)

    @pl.when(i > 0)
    def _():
        sweep([(i, True), (i - 1, False)], True, all_pairs)

    def cond(carry):
        j, rmin = carry
        return jnp.logical_and(j >= 0, rmin < SB_SKIP)

    rmins = [jnp.min(r_ref[p]) for p in all_pairs]
    for p in all_pairs:
        def body(carry, p=p):
            j, _ = carry
            sweep([(j, False)], False, [p])
            return j - 1, jnp.min(r_ref[p])

        lax.while_loop(cond, body, (i - 2, rmins[p]))
    for p in all_pairs:
        acc = acc_ref[p]
        o_ref[:, p * LANES:(p + 1) * LANES] = jnp.where(first, acc[:blk], acc[blk:]).astype(o_ref.dtype)


def _sb_attention(proj_bf, tri, *, batch, seq):
    blk = min(SB_BLOCK, seq)
    nq = seq // blk
    npair = SB_WIDTH // LANES
    t = proj_bf.shape[0]
    return pl.pallas_call(
        functools.partial(_sb_kernel, blk=blk, n_chain=SB_CHAINS, n_pair=npair),
        grid=(batch, nq),
        in_specs=[
            pl.BlockSpec((blk, SB_WIDTH), lambda b, i: (b * nq + i, 0)),
            pl.BlockSpec((seq, SB_WIDTH), lambda b, i: (b, 1)),
            pl.BlockSpec((seq, SB_WIDTH), lambda b, i: (b, 2)),
            pl.BlockSpec((2 * blk, blk), lambda b, i: (0, 0)),
        ],
        out_specs=pl.BlockSpec((blk, SB_WIDTH), lambda b, i: (b * nq + i, 0)),
        out_shape=jax.ShapeDtypeStruct((t, SB_WIDTH), BF16),
        scratch_shapes=[pltpu.VMEM((npair, 2 * blk, LANES), BF16), pltpu.VMEM((npair, 2 * blk, LANES), F32),
                        pltpu.VMEM((npair, 2 * blk, LANES), F32)],
        compiler_params=_params(("parallel", "arbitrary")),
        name="sb_attn",
    )(proj_bf, proj_bf, proj_bf, tri)


def _retention_tables(ch):
    log_gamma = jnp.log1p(-jnp.exp2(-5.0 - jnp.arange(RET_HEADS, dtype=F32)))
    pos = jnp.arange(ch, dtype=F32)
    lg = log_gamma[:, None, None]
    diff = pos[:, None] - pos[None, :]
    decay = jnp.where(diff >= 0, jnp.exp(jnp.maximum(diff, 0.0) * lg), 0.0)
    zeta = jnp.exp((ch - 1 - pos)[None, :] * log_gamma[:, None])
    xi = jnp.exp((pos + 1)[None, :] * log_gamma[:, None])
    cd = jnp.exp(ch * log_gamma)
    xi_b = jnp.broadcast_to(xi[:, :, None], (RET_HEADS, ch, RET_V_DIM))
    zeta_b = jnp.broadcast_to(zeta[:, :, None], (RET_HEADS, ch, RET_QK_DIM))
    cd_b = jnp.broadcast_to(cd[:, None, None], (RET_HEADS, 1, RET_V_DIM))
    return decay, xi_b, zeta_b, cd_b


def _merge_kernel(x_ref, g_ref, wg_ref, bg_ref, osb_ref, wsb_ref, oret_ref, wret_ref, wo_ref, o_ref):
    x = x_ref[...]
    h = _rms(x, g_ref[...]).astype(BF16)
    gate = jax.nn.sigmoid(jnp.dot(h, wg_ref[...], preferred_element_type=F32) + bg_ref[...])
    y_a = jnp.dot(osb_ref[...], wsb_ref[...], preferred_element_type=F32)
    y_b = jnp.dot(oret_ref[...], wret_ref[...], preferred_element_type=F32)
    merged = (gate[:, :D_MODEL] * y_a + gate[:, D_MODEL:] * y_b).astype(BF16)
    o_ref[...] = x + jnp.dot(merged, wo_ref[...], preferred_element_type=F32)


def _merge(x2, g, wg, bg, o_sb, wsb, o_ret, wret, wo, *, tm):
    t = x2.shape[0]
    const = lambda i: (0, 0)
    return pl.pallas_call(
        _merge_kernel,
        grid=(t // tm,),
        in_specs=[
            pl.BlockSpec((tm, D_MODEL), lambda i: (i, 0)),
            pl.BlockSpec((1, D_MODEL), const),
            pl.BlockSpec((D_MODEL, 2 * D_MODEL), const),
            pl.BlockSpec((1, 2 * D_MODEL), const),
            pl.BlockSpec((tm, SB_WIDTH), lambda i: (i, 0)),
            pl.BlockSpec((SB_WIDTH, D_MODEL), const),
            pl.BlockSpec((tm, RET_V_WIDTH), lambda i: (i, 0)),
            pl.BlockSpec((RET_V_WIDTH, D_MODEL), const),
            pl.BlockSpec((D_MODEL, D_MODEL), const),
        ],
        out_specs=pl.BlockSpec((tm, D_MODEL), lambda i: (i, 0)),
        out_shape=jax.ShapeDtypeStruct((t, D_MODEL), F32),
        compiler_params=_params(("parallel",)),
        name="merge",
    )(x2, g, wg, bg, o_sb, wsb, o_ret, wret, wo)


def _ff_chunks():
    chunks, lo = [], 0
    while lo < D_FF:
        w = min(FF_CHUNK, D_FF - lo)
        chunks.append((lo, w))
        lo += w
    return chunks


def _ffn_kernel(x_ref, xp_ref, g_ref, wu_ref, cw_ref, cb_ref, wd_ref, fg_ref, o_ref, abuf_ref,
                *, tm, tiles_per_seq, final):
    i = pl.program_id(0)
    g = g_ref[...]
    x = x_ref[...]
    h = _rms(x, g).astype(BF16)
    keep = jnp.where(lax.rem(i, tiles_per_seq) == 0, 0.0, 1.0)
    hp = (_rms(xp_ref[...], g) * keep).astype(BF16)
    chunks = _ff_chunks()

    def up(c):
        lo, w = chunks[c]
        wa = wu_ref[:, lo:lo + w]
        a = jnp.dot(h, wa, preferred_element_type=F32)
        b = jnp.dot(h, wu_ref[:, D_FF + lo:D_FF + lo + w], preferred_element_type=F32)
        ap = jnp.dot(hp, wa, preferred_element_type=F32)
        return a, b, ap

    nxt = up(0)
    y = x
    for c, (lo, w) in enumerate(chunks):
        a, b, ap = nxt
        if c + 1 < len(chunks):
            nxt = up(c + 1)
        buf = abuf_ref.at[c % 2]
        buf[0:HALO, 0:w] = ap
        buf[HALO:, 0:w] = a
        cw = cw_ref[:, lo:lo + w]
        conv = (buf[HALO - 2:HALO - 2 + tm, 0:w] * cw[0:1, :]
                + buf[HALO - 1:HALO - 1 + tm, 0:w] * cw[1:2, :]
                + a * cw[2:3, :] + cb_ref[:, lo:lo + w])
        act = (conv * jax.nn.sigmoid(conv) * b).astype(BF16)
        y = y + jnp.dot(act, wd_ref[lo:lo + w, :], preferred_element_type=F32)
    if final:
        y = _rms(y, fg_ref[...])
    o_ref[...] = y


def _conv_ffn(x2, g, w_up, cw, cb, wd, fg, *, tm, seq, final):
    t = x2.shape[0]
    hb = tm // HALO
    kern = functools.partial(_ffn_kernel, tm=tm, tiles_per_seq=seq // tm, final=final)
    const = lambda i: (0, 0)
    resident = dict(pipeline_mode=pl.Buffered(1))
    return pl.pallas_call(
        kern,
        grid=(t // tm,),
        in_specs=[
            pl.BlockSpec((tm, D_MODEL), lambda i: (i, 0)),
            pl.BlockSpec((HALO, D_MODEL), lambda i: (jnp.maximum(i * hb - 1, 0), 0)),
            pl.BlockSpec((1, D_MODEL), const),
            pl.BlockSpec((D_MODEL, 2 * D_FF), const, **resident),
            pl.BlockSpec((CONV_WIDTH, D_FF), const),
            pl.BlockSpec((1, D_FF), const),
            pl.BlockSpec((D_FF, D_MODEL), const, **resident),
            pl.BlockSpec((1, D_MODEL), const),
        ],
        out_specs=pl.BlockSpec((tm, D_MODEL), lambda i: (i, 0)),
        out_shape=jax.ShapeDtypeStruct((t, D_MODEL), F32),
        scratch_shapes=[pltpu.VMEM((2, tm + HALO, FF_CHUNK), F32)],
        compiler_params=_params(("parallel",)),
        name="conv_ffn",
    )(x2, x2, g, w_up, cw, cb, wd, fg)


def kernel(x, attn_norm, w_in, b_gate, sb_w_out, ret_norm, ret_w_out, w_o, ffn_norm, w_up, conv_w, conv_b,
           w_down, final_norm):
    batch, seq, d = x.shape
    assert d == D_MODEL
    depth = w_in.shape[0]
    t = batch * seq
    tm = min(512, seq)
    assert seq % tm == 0 and tm % HALO == 0

    half = RET_QK_DIM // 2
    inv = ROPE_BASE ** (-jnp.arange(half, dtype=F32) / half)
    ang = jnp.arange(seq, dtype=F32)[:, None] * inv[None, :]
    cosf = jnp.concatenate([jnp.cos(ang), jnp.cos(ang)], axis=-1)
    sins = jnp.concatenate([-jnp.sin(ang), jnp.sin(ang)], axis=-1)

    ch = min(RET_CHUNK, seq)
    tables = _retention_tables(ch)
    blk = min(SB_BLOCK, seq)
    idx = jnp.arange(blk)
    tri = (idx[:, None] >= idx[None, :]).astype(BF16)
    tri = jnp.concatenate([tri, tri], axis=0)

    x2 = x.reshape(t, d)
    for l in range(depth):
        w_in_bf = w_in[l].astype(BF16)
        g_attn = attn_norm[l][None, :]
        proj_bf, o_ret = _in_proj(x2, g_attn, w_in_bf[:, :PROJ_COLS], cosf, sins, ret_norm[l][None, :], tables,
                                  tm=tm, seq=seq)
        o_sb = _sb_attention(proj_bf, tri, batch=batch, seq=seq)
        x2 = _merge(x2, g_attn, w_in_bf[:, PROJ_COLS:], b_gate[l][None, :], o_sb, sb_w_out[l].astype(BF16),
                    o_ret, ret_w_out[l].astype(BF16), w_o[l].astype(BF16), tm=tm)
        x2 = _conv_ffn(x2, ffn_norm[l][None, :], w_up[l].astype(BF16), conv_w[l], conv_b[l][None, :],
                       w_down[l].astype(BF16), final_norm[None, :], tm=tm, seq=seq,
                       final=(l == depth - 1))
    return x2.reshape(batch, seq, d)
```

```python
import functools

import jax
import jax.numpy as jnp
import numpy as np
from jax import lax
from jax.experimental import pallas as pl
from jax.experimental.pallas import tpu as pltpu

F32 = jnp.float32
BF16 = jnp.bfloat16

LANES = 128
D_MODEL = 1024
SB_HEAD_DIM = 64
SB_WIDTH = 512
RET_HEADS = 4
RET_QK_DIM = 128
RET_V_DIM = 256
RET_QK_WIDTH = RET_HEADS * RET_QK_DIM
RET_V_WIDTH = RET_HEADS * RET_V_DIM
ROPE_BASE = 10000.0
D_FF = 2816
CONV_WIDTH = 3
EPS = 1e-6

PROJ_BF_COLS = 3 * SB_WIDTH + 2 * RET_QK_WIDTH + RET_V_WIDTH
PROJ_COLS = PROJ_BF_COLS + RET_V_WIDTH
PROJ_CHUNK = 512

RET_CHUNK = 256
SB_BLOCK = 256
SB_CHAINS = 2
SB_DEPTH = 2
SB_SKIP = 104.0
SB_MASKED = 1e30
HALO = 16
FF_CHUNK = 1024
GATE_BLOCK = 512

VMEM_LIMIT = 48 * 1024 * 1024


def _rms(x, g):
    ms = jnp.mean(x * x, axis=-1, keepdims=True)
    return x * lax.rsqrt(ms + EPS) * g


def _params(sem):
    return pltpu.CompilerParams(dimension_semantics=sem, vmem_limit_bytes=VMEM_LIMIT)


def _layer_spec(shape, layer, *, col=0, **kw):
    index = (layer,) + (0,) * (len(shape) - 1) + (col,)
    return pl.BlockSpec((None,) + tuple(shape), lambda *_: index, **kw)


def _inproj_kernel(x_ref, g_ref, w_ref, cos_ref, sin_ref, gain_ref, decay_ref, xi_ref, zeta_ref, cd_ref,
                   osb_ref, oret_ref, state_ref, *, tm, ch, tiles_per_seq):
    i = pl.program_id(0)

    @pl.when(lax.rem(i, tiles_per_seq) == 0)
    def _():
        state_ref[...] = jnp.zeros_like(state_ref)

    h = _rms(x_ref[...], g_ref[...]).astype(BF16)
    heads = range(RET_HEADS)
    chunks = range(tm // ch)

    def proj(lo, width):
        return jnp.dot(h, w_ref[:, lo:lo + width], preferred_element_type=F32)

    def rotary(lo, scale):
        acc = proj(lo, RET_QK_WIDTH)
        cos = cos_ref[...]
        sin = sin_ref[...]
        out = []
        for hd in heads:
            t = acc[:, hd * RET_QK_DIM:(hd + 1) * RET_QK_DIM]
            o = t * cos + pltpu.roll(t, RET_QK_DIM // 2, axis=1) * sin
            out.append((o if scale is None else o * scale).astype(BF16))
        return out

    def sb_chunk(c):
        lo = c * PROJ_CHUNK
        acc = proj(lo, PROJ_CHUNK)
        if c == 0:
            acc = acc * (SB_HEAD_DIM ** -0.5)
        osb_ref[:, lo:lo + PROJ_CHUNK] = acc.astype(BF16)

    def rows(a, c):
        return a[c * ch:(c + 1) * ch]

    def vsl(hd):
        return slice(hd * RET_V_DIM, (hd + 1) * RET_V_DIM)

    q_lo = 3 * SB_WIDTH
    q = rotary(q_lo, None)
    k = rotary(q_lo + RET_QK_WIDTH, RET_QK_DIM ** -0.5)
    v_lo = q_lo + 2 * RET_QK_WIDTH
    v = proj(v_lo, RET_V_WIDTH).astype(BF16)
    s = {(c, hd): lax.dot_general(rows(q[hd], c), rows(k[hd], c), (((1,), (1,)), ((), ())),
                                  preferred_element_type=F32) for c in chunks for hd in heads}
    gate = proj(v_lo + RET_V_WIDTH, RET_V_WIDTH)
    sb_order = list(range(3 * SB_WIDTH // PROJ_CHUNK))
    for c in chunks:
        cross = [jnp.dot(rows(q[hd], c), state_ref[hd].astype(BF16), preferred_element_type=F32) for hd in heads]
        kv = []
        for hd in heads:
            kz = (rows(k[hd], c).astype(F32) * zeta_ref[hd]).astype(BF16)
            kv.append(lax.dot_general(kz, rows(v, c)[:, vsl(hd)], (((0,), (0,)), ((), ())),
                                      preferred_element_type=F32))
        inner = [jnp.dot((s[c, hd] * decay_ref[hd]).astype(BF16), rows(v, c)[:, vsl(hd)],
                         preferred_element_type=F32) for hd in heads]
        for hd in heads:
            state_ref[hd] = state_ref[hd] * cd_ref[hd] + kv[hd]
        if sb_order:
            sb_chunk(sb_order.pop(0))
        for hd in heads:
            o = inner[hd] + cross[hd] * xi_ref[hd]
            mu = jnp.mean(o, axis=-1, keepdims=True)
            d = o - mu
            var = jnp.mean(d * d, axis=-1, keepdims=True)
            y = d * lax.rsqrt(var + EPS)
            gt = rows(gate, c)[:, vsl(hd)]
            oret_ref[c * ch:(c + 1) * ch, vsl(hd)] = (
                y * gain_ref[:, vsl(hd)] * (gt * jax.nn.sigmoid(gt))).astype(oret_ref.dtype)
        if sb_order:
            sb_chunk(sb_order.pop(0))
    while sb_order:
        sb_chunk(sb_order.pop(0))


def _in_proj(x2, g, w, cosf, sins, gain, tables, *, layer, tm, seq):
    decay, xi, zeta, cd = tables
    ch = decay.shape[-1]
    t = x2.shape[0]
    spt = seq // tm
    const3 = lambda i: (0, 0, 0)
    resident = dict(pipeline_mode=pl.Buffered(1))
    kern = functools.partial(_inproj_kernel, tm=tm, ch=ch, tiles_per_seq=spt)
    return pl.pallas_call(
        kern,
        grid=(t // tm,),
        in_specs=[
            pl.BlockSpec((tm, D_MODEL), lambda i: (i, 0)),
            _layer_spec((1, D_MODEL), layer),
            _layer_spec((D_MODEL, PROJ_COLS), layer, **resident),
            pl.BlockSpec((tm, LANES), lambda i: (i % spt, 0)),
            pl.BlockSpec((tm, LANES), lambda i: (i % spt, 0)),
            _layer_spec((1, RET_V_WIDTH), layer),
            pl.BlockSpec((RET_HEADS, ch, ch), const3, **resident),
            pl.BlockSpec((RET_HEADS, ch, RET_V_DIM), const3, **resident),
            pl.BlockSpec((RET_HEADS, ch, RET_QK_DIM), const3, **resident),
            pl.BlockSpec((RET_HEADS, 1, RET_V_DIM), const3),
        ],
        out_specs=[
            pl.BlockSpec((tm, 3 * SB_WIDTH), lambda i: (i, 0)),
            pl.BlockSpec((tm, RET_V_WIDTH), lambda i: (i, 0)),
        ],
        out_shape=[
            jax.ShapeDtypeStruct((t, 3 * SB_WIDTH), BF16),
            jax.ShapeDtypeStruct((t, RET_V_WIDTH), BF16),
        ],
        scratch_shapes=[pltpu.VMEM((RET_HEADS, RET_QK_DIM, RET_V_DIM), F32)],
        compiler_params=_params(("arbitrary",)),
        name="in_proj",
    )(x2, g, w, cosf, sins, gain, decay, xi, zeta, cd)


def _sb_kernel(q_ref, k_ref, v_ref, tri_ref, o_ref, qq_ref, acc_ref, r_ref, *, blk, n_chain, n_pair):
    i = pl.program_id(1)
    lane = lax.broadcasted_iota(jnp.int32, (blk, LANES), 1)
    first = lane < SB_HEAD_DIM
    for p in range(n_pair):
        q = q_ref[:, p * LANES:(p + 1) * LANES]
        zero = jnp.zeros_like(q)
        qq_ref[p, 0:blk, :] = jnp.where(first, q, zero)
        qq_ref[p, blk:, :] = jnp.where(first, zero, q)
    rows = 2 * blk // n_chain
    sls = [pl.ds(c * rows, rows) for c in range(n_chain)]

    def sweep(blocks, fresh, pairs):
        tri2 = tri_ref[...]
        starts = [pl.multiple_of(j * blk, blk) for j, _ in blocks]
        units = [(b, p, c) for b in range(len(blocks)) for p in pairs for c in range(n_chain)]
        n = len(units)
        zs, st = {}, {}
        r = {(p, c): None if fresh else r_ref[p, sls[c], :] for p in pairs for c in range(n_chain)}
        o = {(p, c): None for p in pairs for c in range(n_chain)}

        def kv(ref, b, p):
            return ref[pl.ds(starts[b], blk), p * LANES:(p + 1) * LANES]

        def scores(u):
            b, p, c = units[u]
            z = lax.dot_general(qq_ref[p, sls[c], :], kv(k_ref, b, p), (((1,), (1,)), ((), ())),
                                preferred_element_type=F32)
            if blocks[b][1]:
                row = (lax.broadcasted_iota(jnp.int32, z.shape, 0) + c * rows) & (blk - 1)
                col = lax.broadcasted_iota(jnp.int32, z.shape, 1)
                z = jnp.where(col < row, z, -SB_MASKED)
            zs[u] = z

        def softplus_cumsum(u):
            z = zs.pop(u)
            neg_abs = pltpu.bitcast(pltpu.bitcast(z, jnp.uint32) | jnp.uint32(0x80000000), F32)
            sp = jnp.maximum(z, 0.0) + jnp.log(1.0 + jnp.exp(neg_abs))
            hi = sp.astype(BF16)
            lo = (sp - hi.astype(F32)).astype(BF16)
            cs = jnp.dot(jnp.concatenate([hi, lo], axis=1), tri2,
                         preferred_element_type=F32)
            st[u] = (z, cs, jnp.sum(sp, axis=-1, keepdims=True))

        def weights_values(u):
            b, p, c = units[u]
            z, cs, rs = st.pop(u)
            e = z - cs
            rc = r[p, c]
            if rc is not None:
                e = e - (rc if rc.shape[1] == 1 else jnp.concatenate([rc] * (blk // LANES), axis=1))
            part = jnp.dot(jnp.exp(e).astype(BF16), kv(v_ref, b, p), preferred_element_type=F32)
            o[p, c] = part if o[p, c] is None else o[p, c] + part
            r[p, c] = rs if rc is None else rc + rs

        for u in range(min(SB_DEPTH, n)):
            scores(u)
        for u in range(n):
            softplus_cumsum(u)
            if u + SB_DEPTH < n:
                scores(u + SB_DEPTH)
            if u >= 1:
                weights_values(u - 1)
        weights_values(n - 1)
        for p in pairs:
            for c in range(n_chain):
                acc_ref[p, sls[c], :] = o[p, c] if fresh else acc_ref[p, sls[c], :] + o[p, c]
                r_ref[p, sls[c], :] = jnp.broadcast_to(r[p, c], (rows, LANES))

    all_pairs = list(range(n_pair))

    @pl.when(i == 0)
    def _():
        sweep([(i, True)], True, all_pairs)

    @pl.when(i > 0)
    def _():
        sweep([(i, True), (i - 1, False)], True, all_pairs)

    def cond(carry):
        j, rmin = carry
        return jnp.logical_and(j >= 0, rmin < SB_SKIP)

    rmins = [jnp.min(r_ref[p]) for p in all_pairs]
    for p in all_pairs:
        def body(carry, p=p):
            j, _ = carry
            sweep([(j, False)], False, [p])
            return j - 1, jnp.min(r_ref[p])

        lax.while_loop(cond, body, (i - 2, rmins[p]))
    for p in all_pairs:
        acc = acc_ref[p]
        o_ref[:, p * LANES:(p + 1) * LANES] = jnp.where(first, acc[:blk], acc[blk:]).astype(o_ref.dtype)


def _sb_attention(proj_bf, tri, *, batch, seq):
    blk = min(SB_BLOCK, seq)
    nq = seq // blk
    npair = SB_WIDTH // LANES
    t = proj_bf.shape[0]
    return pl.pallas_call(
        functools.partial(_sb_kernel, blk=blk, n_chain=SB_CHAINS, n_pair=npair),
        grid=(batch, nq),
        in_specs=[
            pl.BlockSpec((blk, SB_WIDTH), lambda b, i: (b * nq + i, 0)),
            pl.BlockSpec((seq, SB_WIDTH), lambda b, i: (b, 1)),
            pl.BlockSpec((seq, SB_WIDTH), lambda b, i: (b, 2)),
            pl.BlockSpec((2 * blk, blk), lambda b, i: (0, 0)),
        ],
        out_specs=pl.BlockSpec((blk, SB_WIDTH), lambda b, i: (b * nq + i, 0)),
        out_shape=jax.ShapeDtypeStruct((t, SB_WIDTH), BF16),
        scratch_shapes=[pltpu.VMEM((npair, 2 * blk, LANES), BF16), pltpu.VMEM((npair, 2 * blk, LANES), F32),
                        pltpu.VMEM((npair, 2 * blk, LANES), F32)],
        compiler_params=_params(("parallel", "arbitrary")),
        name="sb_attn",
    )(proj_bf, proj_bf, proj_bf, tri)


def _retention_tables(ch):
    log_gamma = jnp.log1p(-jnp.exp2(-5.0 - jnp.arange(RET_HEADS, dtype=F32)))
    pos = jnp.arange(ch, dtype=F32)
    lg = log_gamma[:, None, None]
    diff = pos[:, None] - pos[None, :]
    decay = jnp.where(diff >= 0, jnp.exp(jnp.maximum(diff, 0.0) * lg), 0.0)
    zeta = jnp.exp((ch - 1 - pos)[None, :] * log_gamma[:, None])
    xi = jnp.exp((pos + 1)[None, :] * log_gamma[:, None])
    cd = jnp.exp(ch * log_gamma)
    xi_b = jnp.broadcast_to(xi[:, :, None], (RET_HEADS, ch, RET_V_DIM))
    zeta_b = jnp.broadcast_to(zeta[:, :, None], (RET_HEADS, ch, RET_QK_DIM))
    cd_b = jnp.broadcast_to(cd[:, None, None], (RET_HEADS, 1, RET_V_DIM))
    return decay, xi_b, zeta_b, cd_b


def _merge_kernel(x_ref, g_ref, *refs):
    n_blk = 2 * D_MODEL // GATE_BLOCK
    wg_refs = refs[:n_blk]
    bg_ref, osb_ref, wsb_ref, oret_ref, wret_ref, wo_ref, o_ref = refs[n_blk:]
    x = x_ref[...]
    h = _rms(x, g_ref[...]).astype(BF16)

    def gate(c):
        z = jnp.dot(h, wg_refs[c][...], preferred_element_type=F32) + bg_ref[:, c * GATE_BLOCK:(c + 1) * GATE_BLOCK]
        return jax.nn.sigmoid(z)

    y_a = jnp.dot(osb_ref[...], wsb_ref[...], preferred_element_type=F32)
    y_b = jnp.dot(oret_ref[...], wret_ref[...], preferred_element_type=F32)
    merged = []
    for c in range(n_blk // 2):
        cols = slice(c * GATE_BLOCK, (c + 1) * GATE_BLOCK)
        merged.append((gate(c) * y_a[:, cols] + gate(n_blk // 2 + c) * y_b[:, cols]).astype(BF16))
    merged = jnp.concatenate(merged, axis=1)
    o_ref[...] = x + jnp.dot(merged, wo_ref[...], preferred_element_type=F32)


def _merge(x2, g, w_in_bf, bg, o_sb, wsb, o_ret, wret, wo, *, layer, tm):
    t = x2.shape[0]
    gate_specs = [_layer_spec((D_MODEL, GATE_BLOCK), layer, col=PROJ_COLS // GATE_BLOCK + c)
                  for c in range(2 * D_MODEL // GATE_BLOCK)]
    return pl.pallas_call(
        _merge_kernel,
        grid=(t // tm,),
        in_specs=[
            pl.BlockSpec((tm, D_MODEL), lambda i: (i, 0)),
            _layer_spec((1, D_MODEL), layer),
            *gate_specs,
            _layer_spec((1, 2 * D_MODEL), layer),
            pl.BlockSpec((tm, SB_WIDTH), lambda i: (i, 0)),
            _layer_spec((SB_WIDTH, D_MODEL), layer),
            pl.BlockSpec((tm, RET_V_WIDTH), lambda i: (i, 0)),
            _layer_spec((RET_V_WIDTH, D_MODEL), layer),
            _layer_spec((D_MODEL, D_MODEL), layer),
        ],
        out_specs=pl.BlockSpec((tm, D_MODEL), lambda i: (i, 0)),
        out_shape=jax.ShapeDtypeStruct((t, D_MODEL), F32),
        compiler_params=_params(("parallel",)),
        name="merge",
    )(x2, g, *([w_in_bf] * len(gate_specs)), bg, o_sb, wsb, o_ret, wret, wo)


def _ff_chunks():
    chunks, lo = [], 0
    while lo < D_FF:
        w = min(FF_CHUNK, D_FF - lo)
        chunks.append((lo, w))
        lo += w
    return chunks


def _ffn_kernel(x_ref, xp_ref, g_ref, wu_ref, cw_ref, cb_ref, wd_ref, fg_ref, o_ref, abuf_ref,
                *, tm, tiles_per_seq, final):
    i = pl.program_id(0)
    g = g_ref[...]
    x = x_ref[...]
    h = _rms(x, g).astype(BF16)
    keep = jnp.where(lax.rem(i, tiles_per_seq) == 0, 0.0, 1.0)
    hp = (_rms(xp_ref[...], g) * keep).astype(BF16)
    chunks = _ff_chunks()

    def up(c):
        lo, w = chunks[c]
        wa = wu_ref[:, lo:lo + w]
        a = jnp.dot(h, wa, preferred_element_type=F32)
        b = jnp.dot(h, wu_ref[:, D_FF + lo:D_FF + lo + w], preferred_element_type=F32)
        ap = jnp.dot(hp, wa, preferred_element_type=F32)
        return a, b, ap

    nxt = up(0)
    y = x
    for c, (lo, w) in enumerate(chunks):
        a, b, ap = nxt
        if c + 1 < len(chunks):
            nxt = up(c + 1)
        buf = abuf_ref.at[c % 2]
        buf[0:HALO, 0:w] = ap
        buf[HALO:, 0:w] = a
        cw = cw_ref[:, lo:lo + w]
        ext = buf[:, 0:w]
        conv = (pltpu.roll(ext, 2, axis=0)[HALO:] * cw[0:1, :]
                + pltpu.roll(ext, 1, axis=0)[HALO:] * cw[1:2, :]
                + a * cw[2:3, :] + cb_ref[:, lo:lo + w])
        act = (conv * jax.nn.sigmoid(conv) * b).astype(BF16)
        y = y + jnp.dot(act, wd_ref[lo:lo + w, :], preferred_element_type=F32)
    if final:
        y = _rms(y, fg_ref[...])
    o_ref[...] = y


def _conv_ffn(x2, g, w_up, cw, cb, wd, fg, *, layer, tm, seq, final):
    t = x2.shape[0]
    hb = tm // HALO
    kern = functools.partial(_ffn_kernel, tm=tm, tiles_per_seq=seq // tm, final=final)
    resident = dict(pipeline_mode=pl.Buffered(1))
    return pl.pallas_call(
        kern,
        grid=(t // tm,),
        in_specs=[
            pl.BlockSpec((tm, D_MODEL), lambda i: (i, 0)),
            pl.BlockSpec((HALO, D_MODEL), lambda i: (jnp.maximum(i * hb - 1, 0), 0)),
            _layer_spec((1, D_MODEL), layer),
            _layer_spec((D_MODEL, 2 * D_FF), layer, **resident),
            _layer_spec((CONV_WIDTH, D_FF), layer),
            _layer_spec((1, D_FF), layer),
            _layer_spec((D_FF, D_MODEL), layer, **resident),
            pl.BlockSpec((1, D_MODEL), lambda i: (0, 0)),
        ],
        out_specs=pl.BlockSpec((tm, D_MODEL), lambda i: (i, 0)),
        out_shape=jax.ShapeDtypeStruct((t, D_MODEL), F32),
        scratch_shapes=[pltpu.VMEM((2, tm + HALO, FF_CHUNK), F32)],
        compiler_params=_params(("parallel",)),
        name="conv_ffn",
    )(x2, x2, g, w_up, cw, cb, wd, fg)


def kernel(x, attn_norm, w_in, b_gate, sb_w_out, ret_norm, ret_w_out, w_o, ffn_norm, w_up, conv_w, conv_b,
           w_down, final_norm):
    batch, seq, d = x.shape
    assert d == D_MODEL
    depth = w_in.shape[0]
    t = batch * seq
    tm = min(512, seq)
    assert seq % tm == 0 and tm % HALO == 0

    half = RET_QK_DIM // 2
    inv = ROPE_BASE ** (-jnp.arange(half, dtype=F32) / half)
    ang = jnp.arange(seq, dtype=F32)[:, None] * inv[None, :]
    cosf = jnp.concatenate([jnp.cos(ang), jnp.cos(ang)], axis=-1)
    sins = jnp.concatenate([-jnp.sin(ang), jnp.sin(ang)], axis=-1)

    ch = min(RET_CHUNK, seq)
    tables = _retention_tables(ch)
    blk = min(SB_BLOCK, seq)
    idx = jnp.arange(blk)
    tri = (idx[:, None] >= idx[None, :]).astype(BF16)
    tri = jnp.concatenate([tri, tri], axis=0)

    def rows(p):
        return p.reshape(depth, 1, p.shape[-1])

    w_in_bf, w_up_bf, w_down_bf = w_in.astype(BF16), w_up.astype(BF16), w_down.astype(BF16)
    sb_w_bf, ret_w_bf, w_o_bf = sb_w_out.astype(BF16), ret_w_out.astype(BF16), w_o.astype(BF16)
    g_attn, g_ffn, g_ret, bg, cb = rows(attn_norm), rows(ffn_norm), rows(ret_norm), rows(b_gate), rows(conv_b)

    x2 = x.reshape(t, d)
    for l in range(depth):
        proj_bf, o_ret = _in_proj(x2, g_attn, w_in_bf, cosf, sins, g_ret, tables, layer=l, tm=tm, seq=seq)
        o_sb = _sb_attention(proj_bf, tri, batch=batch, seq=seq)
        x2 = _merge(x2, g_attn, w_in_bf, bg, o_sb, sb_w_bf, o_ret, ret_w_bf, w_o_bf, layer=l, tm=tm)
        x2 = _conv_ffn(x2, g_ffn, w_up_bf, conv_w, cb, w_down_bf, final_norm[None, :], layer=l, tm=tm, seq=seq,
                       final=(l == depth - 1))
    return x2.reshape(batch, seq, d)
```

```python
import functools

import jax
import jax.numpy as jnp
import numpy as np
from jax import lax
from jax.experimental import pallas as pl
from jax.experimental.pallas import tpu as pltpu

F32 = jnp.float32
BF16 = jnp.bfloat16

LANES = 128
D_MODEL = 1024
SB_HEAD_DIM = 64
SB_WIDTH = 512
RET_HEADS = 4
RET_QK_DIM = 128
RET_V_DIM = 256
RET_QK_WIDTH = RET_HEADS * RET_QK_DIM
RET_V_WIDTH = RET_HEADS * RET_V_DIM
ROPE_BASE = 10000.0
D_FF = 2816
CONV_WIDTH = 3
EPS = 1e-6

PROJ_BF_COLS = 3 * SB_WIDTH + 2 * RET_QK_WIDTH + RET_V_WIDTH
PROJ_COLS = PROJ_BF_COLS + RET_V_WIDTH
PROJ_CHUNK = 512

RET_CHUNK = 256
SB_BLOCK = 256
SB_CHAINS = 2
SB_DEPTH = 2
SB_SKIP = 104.0
SB_MASKED = 1e30
HALO = 16
FF_CHUNK = 1024

VMEM_LIMIT = 48 * 1024 * 1024


def _rms(x, g):
    ms = jnp.mean(x * x, axis=-1, keepdims=True)
    return x * lax.rsqrt(ms + EPS) * g


def _params(sem):
    return pltpu.CompilerParams(dimension_semantics=sem, vmem_limit_bytes=VMEM_LIMIT)


def _layer_spec(shape, layer, *, col=0, **kw):
    index = (layer,) + (0,) * (len(shape) - 1) + (col,)
    return pl.BlockSpec((None,) + tuple(shape), lambda *_: index, **kw)


def _inproj_kernel(x_ref, g_ref, w_ref, cos_ref, sin_ref, gain_ref, decay_ref, xi_ref, zeta_ref, cd_ref,
                   osb_ref, oret_ref, state_ref, *, tm, ch, tiles_per_seq):
    i = pl.program_id(0)

    @pl.when(lax.rem(i, tiles_per_seq) == 0)
    def _():
        state_ref[...] = jnp.zeros_like(state_ref)

    h = _rms(x_ref[...], g_ref[...]).astype(BF16)
    heads = range(RET_HEADS)
    chunks = range(tm // ch)

    def proj(lo, width):
        return jnp.dot(h, w_ref[:, lo:lo + width], preferred_element_type=F32)

    def rotary(lo, scale):
        acc = proj(lo, RET_QK_WIDTH)
        cos = cos_ref[...]
        sin = sin_ref[...]
        out = []
        for hd in heads:
            t = acc[:, hd * RET_QK_DIM:(hd + 1) * RET_QK_DIM]
            o = t * cos + pltpu.roll(t, RET_QK_DIM // 2, axis=1) * sin
            out.append((o if scale is None else o * scale).astype(BF16))
        return out

    def sb_chunk(c):
        lo = c * PROJ_CHUNK
        acc = proj(lo, PROJ_CHUNK)
        if c == 0:
            acc = acc * (SB_HEAD_DIM ** -0.5)
        osb_ref[:, lo:lo + PROJ_CHUNK] = acc.astype(BF16)

    def rows(a, c):
        return a[c * ch:(c + 1) * ch]

    def vsl(hd):
        return slice(hd * RET_V_DIM, (hd + 1) * RET_V_DIM)

    q_lo = 3 * SB_WIDTH
    q = rotary(q_lo, None)
    k = rotary(q_lo + RET_QK_WIDTH, RET_QK_DIM ** -0.5)
    v_lo = q_lo + 2 * RET_QK_WIDTH
    v = proj(v_lo, RET_V_WIDTH).astype(BF16)
    s = {(c, hd): lax.dot_general(rows(q[hd], c), rows(k[hd], c), (((1,), (1,)), ((), ())),
                                  preferred_element_type=F32) for c in chunks for hd in heads}
    gate = proj(v_lo + RET_V_WIDTH, RET_V_WIDTH)
    sb_order = list(range(3 * SB_WIDTH // PROJ_CHUNK))
    for c in chunks:
        cross = [jnp.dot(rows(q[hd], c), state_ref[hd].astype(BF16), preferred_element_type=F32) for hd in heads]
        kv = []
        for hd in heads:
            kz = (rows(k[hd], c).astype(F32) * zeta_ref[hd]).astype(BF16)
            kv.append(lax.dot_general(kz, rows(v, c)[:, vsl(hd)], (((0,), (0,)), ((), ())),
                                      preferred_element_type=F32))
        inner = [jnp.dot((s[c, hd] * decay_ref[hd]).astype(BF16), rows(v, c)[:, vsl(hd)],
                         preferred_element_type=F32) for hd in heads]
        for hd in heads:
            state_ref[hd] = state_ref[hd] * cd_ref[hd] + kv[hd]
        if sb_order:
            sb_chunk(sb_order.pop(0))
        for hd in heads:
            o = inner[hd] + cross[hd] * xi_ref[hd]
            mu = jnp.mean(o, axis=-1, keepdims=True)
            d = o - mu
            var = jnp.mean(d * d, axis=-1, keepdims=True)
            y = d * lax.rsqrt(var + EPS)
            gt = rows(gate, c)[:, vsl(hd)]
            oret_ref[c * ch:(c + 1) * ch, vsl(hd)] = (
                y * gain_ref[:, vsl(hd)] * (gt * jax.nn.sigmoid(gt))).astype(oret_ref.dtype)
        if sb_order:
            sb_chunk(sb_order.pop(0))
    while sb_order:
        sb_chunk(sb_order.pop(0))


def _in_proj(x2, g, w, cosf, sins, gain, tables, *, layer, tm, seq):
    decay, xi, zeta, cd = tables
    ch = decay.shape[-1]
    t = x2.shape[0]
    spt = seq // tm
    const3 = lambda i: (0, 0, 0)
    resident = dict(pipeline_mode=pl.Buffered(1))
    kern = functools.partial(_inproj_kernel, tm=tm, ch=ch, tiles_per_seq=spt)
    return pl.pallas_call(
        kern,
        grid=(t // tm,),
        in_specs=[
            pl.BlockSpec((tm, D_MODEL), lambda i: (i, 0)),
            _layer_spec((1, D_MODEL), layer),
            _layer_spec((D_MODEL, PROJ_COLS), layer, **resident),
            pl.BlockSpec((tm, LANES), lambda i: (i % spt, 0)),
            pl.BlockSpec((tm, LANES), lambda i: (i % spt, 0)),
            _layer_spec((1, RET_V_WIDTH), layer),
            pl.BlockSpec((RET_HEADS, ch, ch), const3, **resident),
            pl.BlockSpec((RET_HEADS, ch, RET_V_DIM), const3, **resident),
            pl.BlockSpec((RET_HEADS, ch, RET_QK_DIM), const3, **resident),
            pl.BlockSpec((RET_HEADS, 1, RET_V_DIM), const3),
        ],
        out_specs=[
            pl.BlockSpec((tm, 3 * SB_WIDTH), lambda i: (i, 0)),
            pl.BlockSpec((tm, RET_V_WIDTH), lambda i: (i, 0)),
        ],
        out_shape=[
            jax.ShapeDtypeStruct((t, 3 * SB_WIDTH), BF16),
            jax.ShapeDtypeStruct((t, RET_V_WIDTH), BF16),
        ],
        scratch_shapes=[pltpu.VMEM((RET_HEADS, RET_QK_DIM, RET_V_DIM), F32)],
        compiler_params=_params(("arbitrary",)),
        name="in_proj",
    )(x2, g, w, cosf, sins, gain, decay, xi, zeta, cd)


def _sb_kernel(q_ref, k_ref, v_ref, tri_ref, o_ref, qq_ref, acc_ref, r_ref, *, blk, n_chain, n_pair):
    i = pl.program_id(1)
    lane = lax.broadcasted_iota(jnp.int32, (blk, LANES), 1)
    first = lane < SB_HEAD_DIM
    for p in range(n_pair):
        q = q_ref[:, p * LANES:(p + 1) * LANES]
        zero = jnp.zeros_like(q)
        qq_ref[p, 0:blk, :] = jnp.where(first, q, zero)
        qq_ref[p, blk:, :] = jnp.where(first, zero, q)
    rows = 2 * blk // n_chain
    sls = [pl.ds(c * rows, rows) for c in range(n_chain)]

    def sweep(blocks, fresh, pairs):
        tri = tri_ref[...]
        starts = [pl.multiple_of(j * blk, blk) for j, _ in blocks]
        units = [(b, p, c) for b in range(len(blocks)) for p in pairs for c in range(n_chain)]
        n = len(units)
        zs, st = {}, {}
        r = {(p, c): None if fresh else r_ref[p, sls[c], :] for p in pairs for c in range(n_chain)}
        o = {(p, c): None for p in pairs for c in range(n_chain)}

        def kv(ref, b, p):
            return ref[pl.ds(starts[b], blk), p * LANES:(p + 1) * LANES]

        def scores(u):
            b, p, c = units[u]
            z = lax.dot_general(qq_ref[p, sls[c], :], kv(k_ref, b, p), (((1,), (1,)), ((), ())),
                                preferred_element_type=F32)
            if blocks[b][1]:
                row = (lax.broadcasted_iota(jnp.int32, z.shape, 0) + c * rows) & (blk - 1)
                col = lax.broadcasted_iota(jnp.int32, z.shape, 1)
                z = jnp.where(col < row, z, -SB_MASKED)
            zs[u] = z

        def softplus_cumsum(u):
            z = zs.pop(u)
            neg_abs = pltpu.bitcast(pltpu.bitcast(z, jnp.uint32) | jnp.uint32(0x80000000), F32)
            sp = jnp.maximum(z, 0.0) + jnp.log(1.0 + jnp.exp(neg_abs))
            cs = jnp.dot(sp.astype(BF16), tri, preferred_element_type=F32)
            st[u] = (z, cs, jnp.sum(sp, axis=-1, keepdims=True))

        def weights_values(u):
            b, p, c = units[u]
            z, cs, rs = st.pop(u)
            e = z - cs
            rc = r[p, c]
            if rc is not None:
                e = e - (rc if rc.shape[1] == 1 else jnp.concatenate([rc] * (blk // LANES), axis=1))
            part = jnp.dot(jnp.exp(e).astype(BF16), kv(v_ref, b, p), preferred_element_type=F32)
            o[p, c] = part if o[p, c] is None else o[p, c] + part
            r[p, c] = rs if rc is None else rc + rs

        for u in range(min(SB_DEPTH, n)):
            scores(u)
        for u in range(n):
            softplus_cumsum(u)
            if u + SB_DEPTH < n:
                scores(u + SB_DEPTH)
            if u >= 1:
                weights_values(u - 1)
        weights_values(n - 1)
        for p in pairs:
            for c in range(n_chain):
                acc_ref[p, sls[c], :] = o[p, c] if fresh else acc_ref[p, sls[c], :] + o[p, c]
                r_ref[p, sls[c], :] = jnp.broadcast_to(r[p, c], (rows, LANES))

    all_pairs = list(range(n_pair))

    @pl.when(i == 0)
    def _():
        sweep([(i, True)], True, all_pairs)

    @pl.when(i > 0)
    def _():
        sweep([(i, True), (i - 1, False)], True, all_pairs)

    def cond(carry):
        j, rmin = carry
        return jnp.logical_and(j >= 0, rmin < SB_SKIP)

    rmins = [jnp.min(r_ref[p]) for p in all_pairs]
    for p in all_pairs:
        def body(carry, p=p):
            j, _ = carry
            sweep([(j, False)], False, [p])
            return j - 1, jnp.min(r_ref[p])

        lax.while_loop(cond, body, (i - 2, rmins[p]))
    for p in all_pairs:
        acc = acc_ref[p]
        o_ref[:, p * LANES:(p + 1) * LANES] = jnp.where(first, acc[:blk], acc[blk:]).astype(o_ref.dtype)


def _sb_attention(proj_bf, tri, *, batch, seq):
    blk = min(SB_BLOCK, seq)
    nq = seq // blk
    npair = SB_WIDTH // LANES
    t = proj_bf.shape[0]
    return pl.pallas_call(
        functools.partial(_sb_kernel, blk=blk, n_chain=SB_CHAINS, n_pair=npair),
        grid=(batch, nq),
        in_specs=[
            pl.BlockSpec((blk, SB_WIDTH), lambda b, i: (b * nq + i, 0)),
            pl.BlockSpec((seq, SB_WIDTH), lambda b, i: (b, 1)),
            pl.BlockSpec((seq, SB_WIDTH), lambda b, i: (b, 2)),
            pl.BlockSpec((blk, blk), lambda b, i: (0, 0)),
        ],
        out_specs=pl.BlockSpec((blk, SB_WIDTH), lambda b, i: (b * nq + i, 0)),
        out_shape=jax.ShapeDtypeStruct((t, SB_WIDTH), BF16),
        scratch_shapes=[pltpu.VMEM((npair, 2 * blk, LANES), BF16), pltpu.VMEM((npair, 2 * blk, LANES), F32),
                        pltpu.VMEM((npair, 2 * blk, LANES), F32)],
        compiler_params=_params(("parallel", "arbitrary")),
        name="sb_attn",
    )(proj_bf, proj_bf, proj_bf, tri)


def _retention_tables(ch):
    log_gamma = jnp.log1p(-jnp.exp2(-5.0 - jnp.arange(RET_HEADS, dtype=F32)))
    pos = jnp.arange(ch, dtype=F32)
    lg = log_gamma[:, None, None]
    diff = pos[:, None] - pos[None, :]
    decay = jnp.where(diff >= 0, jnp.exp(jnp.maximum(diff, 0.0) * lg), 0.0)
    zeta = jnp.exp((ch - 1 - pos)[None, :] * log_gamma[:, None])
    xi = jnp.exp((pos + 1)[None, :] * log_gamma[:, None])
    cd = jnp.exp(ch * log_gamma)
    xi_b = jnp.broadcast_to(xi[:, :, None], (RET_HEADS, ch, RET_V_DIM))
    zeta_b = jnp.broadcast_to(zeta[:, :, None], (RET_HEADS, ch, RET_QK_DIM))
    cd_b = jnp.broadcast_to(cd[:, None, None], (RET_HEADS, 1, RET_V_DIM))
    return decay, xi_b, zeta_b, cd_b


def _merge_kernel(x_ref, g_ref, wg_ref, bg_ref, osb_ref, wsb_ref, oret_ref, wret_ref, wo_ref, o_ref):
    x = x_ref[...]
    h = _rms(x, g_ref[...]).astype(BF16)
    gate = jax.nn.sigmoid(jnp.dot(h, wg_ref[...], preferred_element_type=F32) + bg_ref[...])
    y_a = jnp.dot(osb_ref[...], wsb_ref[...], preferred_element_type=F32)
    y_b = jnp.dot(oret_ref[...], wret_ref[...], preferred_element_type=F32)
    merged = (gate[:, :D_MODEL] * y_a + gate[:, D_MODEL:] * y_b).astype(BF16)
    o_ref[...] = x + jnp.dot(merged, wo_ref[...], preferred_element_type=F32)


def _merge(x2, g, wg, bg, o_sb, wsb, o_ret, wret, wo, *, layer, tm):
    t = x2.shape[0]
    return pl.pallas_call(
        _merge_kernel,
        grid=(t // tm,),
        in_specs=[
            pl.BlockSpec((tm, D_MODEL), lambda i: (i, 0)),
            _layer_spec((1, D_MODEL), layer),
            _layer_spec((D_MODEL, 2 * D_MODEL), layer),
            _layer_spec((1, 2 * D_MODEL), layer),
            pl.BlockSpec((tm, SB_WIDTH), lambda i: (i, 0)),
            _layer_spec((SB_WIDTH, D_MODEL), layer),
            pl.BlockSpec((tm, RET_V_WIDTH), lambda i: (i, 0)),
            _layer_spec((RET_V_WIDTH, D_MODEL), layer),
            _layer_spec((D_MODEL, D_MODEL), layer),
        ],
        out_specs=pl.BlockSpec((tm, D_MODEL), lambda i: (i, 0)),
        out_shape=jax.ShapeDtypeStruct((t, D_MODEL), F32),
        compiler_params=_params(("parallel",)),
        name="merge",
    )(x2, g, wg, bg, o_sb, wsb, o_ret, wret, wo)


def _ff_chunks():
    chunks, lo = [], 0
    while lo < D_FF:
        w = min(FF_CHUNK, D_FF - lo)
        chunks.append((lo, w))
        lo += w
    return chunks


def _ffn_kernel(x_ref, xp_ref, g_ref, wu_ref, cw_ref, cb_ref, wd_ref, fg_ref, o_ref, abuf_ref,
                *, tm, tiles_per_seq, final):
    i = pl.program_id(0)
    g = g_ref[...]
    x = x_ref[...]
    h = _rms(x, g).astype(BF16)
    keep = jnp.where(lax.rem(i, tiles_per_seq) == 0, 0.0, 1.0)
    hp = (_rms(xp_ref[...], g) * keep).astype(BF16)
    chunks = _ff_chunks()

    def up(c):
        lo, w = chunks[c]
        wa = wu_ref[:, lo:lo + w]
        a = jnp.dot(h, wa, preferred_element_type=F32)
        b = jnp.dot(h, wu_ref[:, D_FF + lo:D_FF + lo + w], preferred_element_type=F32)
        ap = jnp.dot(hp, wa, preferred_element_type=F32)
        return a, b, ap

    nxt = up(0)
    y = x
    for c, (lo, w) in enumerate(chunks):
        a, b, ap = nxt
        if c + 1 < len(chunks):
            nxt = up(c + 1)
        buf = abuf_ref.at[c % 2]
        buf[0:HALO, 0:w] = ap
        buf[HALO:, 0:w] = a
        cw = cw_ref[:, lo:lo + w]
        ext = buf[:, 0:w]
        conv = (pltpu.roll(ext, 2, axis=0)[HALO:] * cw[0:1, :]
                + pltpu.roll(ext, 1, axis=0)[HALO:] * cw[1:2, :]
                + a * cw[2:3, :] + cb_ref[:, lo:lo + w])
        act = (conv * jax.nn.sigmoid(conv) * b).astype(BF16)
        y = y + jnp.dot(act, wd_ref[lo:lo + w, :], preferred_element_type=F32)
    if final:
        y = _rms(y, fg_ref[...])
    o_ref[...] = y


def _conv_ffn(x2, g, w_up, cw, cb, wd, fg, *, layer, tm, seq, final):
    t = x2.shape[0]
    hb = tm // HALO
    kern = functools.partial(_ffn_kernel, tm=tm, tiles_per_seq=seq // tm, final=final)
    resident = dict(pipeline_mode=pl.Buffered(1))
    return pl.pallas_call(
        kern,
        grid=(t // tm,),
        in_specs=[
            pl.BlockSpec((tm, D_MODEL), lambda i: (i, 0)),
            pl.BlockSpec((HALO, D_MODEL), lambda i: (jnp.maximum(i * hb - 1, 0), 0)),
            _layer_spec((1, D_MODEL), layer),
            _layer_spec((D_MODEL, 2 * D_FF), layer, **resident),
            _layer_spec((CONV_WIDTH, D_FF), layer),
            _layer_spec((1, D_FF), layer),
            _layer_spec((D_FF, D_MODEL), layer, **resident),
            pl.BlockSpec((1, D_MODEL), lambda i: (0, 0)),
        ],
        out_specs=pl.BlockSpec((tm, D_MODEL), lambda i: (i, 0)),
        out_shape=jax.ShapeDtypeStruct((t, D_MODEL), F32),
        scratch_shapes=[pltpu.VMEM((2, tm + HALO, FF_CHUNK), F32)],
        compiler_params=_params(("parallel",)),
        name="conv_ffn",
    )(x2, x2, g, w_up, cw, cb, wd, fg)


def kernel(x, attn_norm, w_in, b_gate, sb_w_out, ret_norm, ret_w_out, w_o, ffn_norm, w_up, conv_w, conv_b,
           w_down, final_norm):
    batch, seq, d = x.shape
    assert d == D_MODEL
    depth = w_in.shape[0]
    t = batch * seq
    tm = min(512, seq)
    assert seq % tm == 0 and tm % HALO == 0

    half = RET_QK_DIM // 2
    inv = ROPE_BASE ** (-jnp.arange(half, dtype=F32) / half)
    ang = jnp.arange(seq, dtype=F32)[:, None] * inv[None, :]
    cosf = jnp.concatenate([jnp.cos(ang), jnp.cos(ang)], axis=-1)
    sins = jnp.concatenate([-jnp.sin(ang), jnp.sin(ang)], axis=-1)

    ch = min(RET_CHUNK, seq)
    tables = _retention_tables(ch)
    blk = min(SB_BLOCK, seq)
    idx = jnp.arange(blk)
    tri = (idx[:, None] >= idx[None, :]).astype(BF16)

    def rows(p):
        return p.reshape(depth, 1, p.shape[-1])

    w_in_bf, w_up_bf, w_down_bf = w_in.astype(BF16), w_up.astype(BF16), w_down.astype(BF16)
    w_gate_bf = w_in_bf[:, :, PROJ_COLS:]
    sb_w_bf, ret_w_bf, w_o_bf = sb_w_out.astype(BF16), ret_w_out.astype(BF16), w_o.astype(BF16)
    g_attn, g_ffn, g_ret, bg, cb = rows(attn_norm), rows(ffn_norm), rows(ret_norm), rows(b_gate), rows(conv_b)

    x2 = x.reshape(t, d)
    for l in range(depth):
        proj_bf, o_ret = _in_proj(x2, g_attn, w_in_bf, cosf, sins, g_ret, tables, layer=l, tm=tm, seq=seq)
        o_sb = _sb_attention(proj_bf, tri, batch=batch, seq=seq)
        x2 = _merge(x2, g_attn, w_gate_bf, bg, o_sb, sb_w_bf, o_ret, ret_w_bf, w_o_bf, layer=l, tm=tm)
        x2 = _conv_ffn(x2, g_ffn, w_up_bf, conv_w, cb, w_down_bf, final_norm[None, :], layer=l, tm=tm, seq=seq,
                       final=(l == depth - 1))
    return x2.reshape(batch, seq, d)
```

```python
import functools

import jax
import jax.numpy as jnp
import numpy as np
from jax import lax
from jax.experimental import pallas as pl
from jax.experimental.pallas import tpu as pltpu

F32 = jnp.float32
BF16 = jnp.bfloat16

LANES = 128
D_MODEL = 1024
SB_HEAD_DIM = 64
SB_WIDTH = 512
RET_HEADS = 4
RET_QK_DIM = 128
RET_V_DIM = 256
RET_QK_WIDTH = RET_HEADS * RET_QK_DIM
RET_V_WIDTH = RET_HEADS * RET_V_DIM
ROPE_BASE = 10000.0
D_FF = 2816
CONV_WIDTH = 3
EPS = 1e-6

PROJ_BF_COLS = 3 * SB_WIDTH + 2 * RET_QK_WIDTH + RET_V_WIDTH
PROJ_COLS = PROJ_BF_COLS + RET_V_WIDTH
PROJ_CHUNK = 512

RET_CHUNK = 256
SB_BLOCK = 256
SB_CHAINS = 2
SB_DEPTH = 2
SB_SKIP = 104.0
SB_MASKED = 1e30
HALO = 16
FF_CHUNK = 1024

VMEM_LIMIT = 48 * 1024 * 1024


def _rms(x, g):
    ms = jnp.mean(x * x, axis=-1, keepdims=True)
    return x * lax.rsqrt(ms + EPS) * g


def _params(sem):
    return pltpu.CompilerParams(dimension_semantics=sem, vmem_limit_bytes=VMEM_LIMIT)


def _layer_spec(shape, layer, *, col=0, **kw):
    index = (layer,) + (0,) * (len(shape) - 1) + (col,)
    return pl.BlockSpec((None,) + tuple(shape), lambda *_: index, **kw)


def _inproj_kernel(x_ref, g_ref, w_ref, cos_ref, sin_ref, gain_ref, decay_ref, xi_ref, zeta_ref, cd_ref,
                   osb_ref, oret_ref, state_ref, *, tm, ch, tiles_per_seq):
    i = pl.program_id(0)

    @pl.when(lax.rem(i, tiles_per_seq) == 0)
    def _():
        state_ref[...] = jnp.zeros_like(state_ref)

    h = _rms(x_ref[...], g_ref[...]).astype(BF16)
    heads = range(RET_HEADS)
    chunks = range(tm // ch)

    def proj(lo, width):
        return jnp.dot(h, w_ref[:, lo:lo + width], preferred_element_type=F32)

    def rotary(lo, scale):
        acc = proj(lo, RET_QK_WIDTH)
        cos = cos_ref[...]
        sin = sin_ref[...]
        out = []
        for hd in heads:
            t = acc[:, hd * RET_QK_DIM:(hd + 1) * RET_QK_DIM]
            o = t * cos + pltpu.roll(t, RET_QK_DIM // 2, axis=1) * sin
            out.append((o if scale is None else o * scale).astype(BF16))
        return out

    def sb_chunk(c):
        lo = c * PROJ_CHUNK
        acc = proj(lo, PROJ_CHUNK)
        if c == 0:
            acc = acc * (SB_HEAD_DIM ** -0.5)
        osb_ref[:, lo:lo + PROJ_CHUNK] = acc.astype(BF16)

    def rows(a, c):
        return a[c * ch:(c + 1) * ch]

    def vsl(hd):
        return slice(hd * RET_V_DIM, (hd + 1) * RET_V_DIM)

    q_lo = 3 * SB_WIDTH
    q = rotary(q_lo, None)
    k = rotary(q_lo + RET_QK_WIDTH, RET_QK_DIM ** -0.5)
    v_lo = q_lo + 2 * RET_QK_WIDTH
    v = proj(v_lo, RET_V_WIDTH).astype(BF16)
    s = {(c, hd): lax.dot_general(rows(q[hd], c), rows(k[hd], c), (((1,), (1,)), ((), ())),
                                  preferred_element_type=F32) for c in chunks for hd in heads}
    gate = None
    sb_order = list(range(3 * SB_WIDTH // PROJ_CHUNK))
    for c in chunks:
        cross = [jnp.dot(rows(q[hd], c), state_ref[hd].astype(BF16), preferred_element_type=F32) for hd in heads]
        kv = []
        for hd in heads:
            kz = (rows(k[hd], c).astype(F32) * zeta_ref[hd]).astype(BF16)
            kv.append(lax.dot_general(kz, rows(v, c)[:, vsl(hd)], (((0,), (0,)), ((), ())),
                                      preferred_element_type=F32))
        inner = [jnp.dot((s[c, hd] * decay_ref[hd]).astype(BF16), rows(v, c)[:, vsl(hd)],
                         preferred_element_type=F32) for hd in heads]
        for hd in heads:
            state_ref[hd] = state_ref[hd] * cd_ref[hd] + kv[hd]
        if gate is None:
            gate = proj(v_lo + RET_V_WIDTH, RET_V_WIDTH)
        elif sb_order:
            sb_chunk(sb_order.pop(0))
        for hd in heads:
            o = inner[hd] + cross[hd] * xi_ref[hd]
            mu = jnp.mean(o, axis=-1, keepdims=True)
            d = o - mu
            var = jnp.mean(d * d, axis=-1, keepdims=True)
            y = d * lax.rsqrt(var + EPS)
            gt = rows(gate, c)[:, vsl(hd)]
            oret_ref[c * ch:(c + 1) * ch, vsl(hd)] = (
                y * gain_ref[:, vsl(hd)] * (gt * jax.nn.sigmoid(gt))).astype(oret_ref.dtype)
        if sb_order:
            sb_chunk(sb_order.pop(0))
    while sb_order:
        sb_chunk(sb_order.pop(0))


def _in_proj(x2, g, w, cosf, sins, gain, tables, *, layer, tm, seq):
    decay, xi, zeta, cd = tables
    ch = decay.shape[-1]
    t = x2.shape[0]
    spt = seq // tm
    const3 = lambda i: (0, 0, 0)
    resident = dict(pipeline_mode=pl.Buffered(1))
    kern = functools.partial(_inproj_kernel, tm=tm, ch=ch, tiles_per_seq=spt)
    return pl.pallas_call(
        kern,
        grid=(t // tm,),
        in_specs=[
            pl.BlockSpec((tm, D_MODEL), lambda i: (i, 0)),
            _layer_spec((1, D_MODEL), layer),
            _layer_spec((D_MODEL, PROJ_COLS), layer, **resident),
            pl.BlockSpec((tm, LANES), lambda i: (i % spt, 0)),
            pl.BlockSpec((tm, LANES), lambda i: (i % spt, 0)),
            _layer_spec((1, RET_V_WIDTH), layer),
            pl.BlockSpec((RET_HEADS, ch, ch), const3, **resident),
            pl.BlockSpec((RET_HEADS, ch, RET_V_DIM), const3, **resident),
            pl.BlockSpec((RET_HEADS, ch, RET_QK_DIM), const3, **resident),
            pl.BlockSpec((RET_HEADS, 1, RET_V_DIM), const3),
        ],
        out_specs=[
            pl.BlockSpec((tm, 3 * SB_WIDTH), lambda i: (i, 0)),
            pl.BlockSpec((tm, RET_V_WIDTH), lambda i: (i, 0)),
        ],
        out_shape=[
            jax.ShapeDtypeStruct((t, 3 * SB_WIDTH), BF16),
            jax.ShapeDtypeStruct((t, RET_V_WIDTH), BF16),
        ],
        scratch_shapes=[pltpu.VMEM((RET_HEADS, RET_QK_DIM, RET_V_DIM), F32)],
        compiler_params=_params(("arbitrary",)),
        name="in_proj",
    )(x2, g, w, cosf, sins, gain, decay, xi, zeta, cd)


def _sb_kernel(q_ref, k_ref, v_ref, tri_ref, o_ref, qq_ref, acc_ref, r_ref, *, blk, n_chain, n_pair):
    i = pl.program_id(1)
    lane = lax.broadcasted_iota(jnp.int32, (blk, LANES), 1)
    first = lane < SB_HEAD_DIM
    for p in range(n_pair):
        q = q_ref[:, p * LANES:(p + 1) * LANES]
        zero = jnp.zeros_like(q)
        qq_ref[p, 0:blk, :] = jnp.where(first, q, zero)
        qq_ref[p, blk:, :] = jnp.where(first, zero, q)
    rows = 2 * blk // n_chain
    sls = [pl.ds(c * rows, rows) for c in range(n_chain)]

    def sweep(blocks, fresh, pairs):
        tri = tri_ref[...]
        starts = [pl.multiple_of(j * blk, blk) for j, _ in blocks]
        units = [(b, p, c) for b in range(len(blocks)) for p in pairs for c in range(n_chain)]
        n = len(units)
        zs, st = {}, {}
        r = {(p, c): None if fresh else r_ref[p, sls[c], :] for p in pairs for c in range(n_chain)}
        o = {(p, c): None for p in pairs for c in range(n_chain)}

        def kv(ref, b, p):
            return ref[pl.ds(starts[b], blk), p * LANES:(p + 1) * LANES]

        def scores(u):
            b, p, c = units[u]
            z = lax.dot_general(qq_ref[p, sls[c], :], kv(k_ref, b, p), (((1,), (1,)), ((), ())),
                                preferred_element_type=F32)
            if blocks[b][1]:
                row = (lax.broadcasted_iota(jnp.int32, z.shape, 0) + c * rows) & (blk - 1)
                col = lax.broadcasted_iota(jnp.int32, z.shape, 1)
                z = jnp.where(col < row, z, -SB_MASKED)
            zs[u] = z

        def softplus_cumsum(u):
            z = zs.pop(u)
            neg_abs = pltpu.bitcast(pltpu.bitcast(z, jnp.uint32) | jnp.uint32(0x80000000), F32)
            sp = jnp.maximum(z, 0.0) + jnp.log(1.0 + jnp.exp(neg_abs))
            cs = jnp.dot(sp.astype(BF16), tri, preferred_element_type=F32)
            st[u] = (z, cs, jnp.sum(sp, axis=-1, keepdims=True))

        def weights_values(u):
            b, p, c = units[u]
            z, cs, rs = st.pop(u)
            e = z - cs
            rc = r[p, c]
            if rc is not None:
                e = e - (rc if rc.shape[1] == 1 else jnp.concatenate([rc] * (blk // LANES), axis=1))
            part = jnp.dot(jnp.exp(e).astype(BF16), kv(v_ref, b, p), preferred_element_type=F32)
            o[p, c] = part if o[p, c] is None else o[p, c] + part
            r[p, c] = rs if rc is None else rc + rs

        for u in range(min(SB_DEPTH, n)):
            scores(u)
        for u in range(n):
            softplus_cumsum(u)
            if u + SB_DEPTH < n:
                scores(u + SB_DEPTH)
            if u >= 1:
                weights_values(u - 1)
        weights_values(n - 1)
        for p in pairs:
            for c in range(n_chain):
                acc_ref[p, sls[c], :] = o[p, c] if fresh else acc_ref[p, sls[c], :] + o[p, c]
                r_ref[p, sls[c], :] = jnp.broadcast_to(r[p, c], (rows, LANES))

    all_pairs = list(range(n_pair))

    @pl.when(i == 0)
    def _():
        sweep([(i, True)], True, all_pairs)

    @pl.when(i > 0)
    def _():
        sweep([(i, True), (i - 1, False)], True, all_pairs)

    def cond(carry):
        j, rmin = carry
        return jnp.logical_and(j >= 0, rmin < SB_SKIP)

    rmins = [jnp.min(r_ref[p]) for p in all_pairs]
    for p in all_pairs:
        def body(carry, p=p):
            j, _ = carry
            sweep([(j, False)], False, [p])
            return j - 1, jnp.min(r_ref[p])

        lax.while_loop(cond, body, (i - 2, rmins[p]))
    for p in all_pairs:
        acc = acc_ref[p]
        o_ref[:, p * LANES:(p + 1) * LANES] = jnp.where(first, acc[:blk], acc[blk:]).astype(o_ref.dtype)


def _sb_attention(proj_bf, tri, *, batch, seq):
    blk = min(SB_BLOCK, seq)
    nq = seq // blk
    npair = SB_WIDTH // LANES
    t = proj_bf.shape[0]
    return pl.pallas_call(
        functools.partial(_sb_kernel, blk=blk, n_chain=SB_CHAINS, n_pair=npair),
        grid=(batch, nq),
        in_specs=[
            pl.BlockSpec((blk, SB_WIDTH), lambda b, i: (b * nq + i, 0)),
            pl.BlockSpec((seq, SB_WIDTH), lambda b, i: (b, 1)),
            pl.BlockSpec((seq, SB_WIDTH), lambda b, i: (b, 2)),
            pl.BlockSpec((blk, blk), lambda b, i: (0, 0)),
        ],
        out_specs=pl.BlockSpec((blk, SB_WIDTH), lambda b, i: (b * nq + i, 0)),
        out_shape=jax.ShapeDtypeStruct((t, SB_WIDTH), BF16),
        scratch_shapes=[pltpu.VMEM((npair, 2 * blk, LANES), BF16), pltpu.VMEM((npair, 2 * blk, LANES), F32),
                        pltpu.VMEM((npair, 2 * blk, LANES), F32)],
        compiler_params=_params(("parallel", "arbitrary")),
        name="sb_attn",
    )(proj_bf, proj_bf, proj_bf, tri)


def _retention_tables(ch):
    log_gamma = jnp.log1p(-jnp.exp2(-5.0 - jnp.arange(RET_HEADS, dtype=F32)))
    pos = jnp.arange(ch, dtype=F32)
    lg = log_gamma[:, None, None]
    diff = pos[:, None] - pos[None, :]
    decay = jnp.where(diff >= 0, jnp.exp(jnp.maximum(diff, 0.0) * lg), 0.0)
    zeta = jnp.exp((ch - 1 - pos)[None, :] * log_gamma[:, None])
    xi = jnp.exp((pos + 1)[None, :] * log_gamma[:, None])
    cd = jnp.exp(ch * log_gamma)
    xi_b = jnp.broadcast_to(xi[:, :, None], (RET_HEADS, ch, RET_V_DIM))
    zeta_b = jnp.broadcast_to(zeta[:, :, None], (RET_HEADS, ch, RET_QK_DIM))
    cd_b = jnp.broadcast_to(cd[:, None, None], (RET_HEADS, 1, RET_V_DIM))
    return decay, xi_b, zeta_b, cd_b


def _merge_kernel(x_ref, g_ref, wg_ref, bg_ref, osb_ref, wsb_ref, oret_ref, wret_ref, wo_ref, o_ref):
    x = x_ref[...]
    h = _rms(x, g_ref[...]).astype(BF16)
    gate = jax.nn.sigmoid(jnp.dot(h, wg_ref[...], preferred_element_type=F32) + bg_ref[...])
    y_a = jnp.dot(osb_ref[...], wsb_ref[...], preferred_element_type=F32)
    y_b = jnp.dot(oret_ref[...], wret_ref[...], preferred_element_type=F32)
    merged = (gate[:, :D_MODEL] * y_a + gate[:, D_MODEL:] * y_b).astype(BF16)
    o_ref[...] = x + jnp.dot(merged, wo_ref[...], preferred_element_type=F32)


def _merge(x2, g, wg, bg, o_sb, wsb, o_ret, wret, wo, *, layer, tm):
    t = x2.shape[0]
    return pl.pallas_call(
        _merge_kernel,
        grid=(t // tm,),
        in_specs=[
            pl.BlockSpec((tm, D_MODEL), lambda i: (i, 0)),
            _layer_spec((1, D_MODEL), layer),
            _layer_spec((D_MODEL, 2 * D_MODEL), layer),
            _layer_spec((1, 2 * D_MODEL), layer),
            pl.BlockSpec((tm, SB_WIDTH), lambda i: (i, 0)),
            _layer_spec((SB_WIDTH, D_MODEL), layer),
            pl.BlockSpec((tm, RET_V_WIDTH), lambda i: (i, 0)),
            _layer_spec((RET_V_WIDTH, D_MODEL), layer),
            _layer_spec((D_MODEL, D_MODEL), layer),
        ],
        out_specs=pl.BlockSpec((tm, D_MODEL), lambda i: (i, 0)),
        out_shape=jax.ShapeDtypeStruct((t, D_MODEL), F32),
        compiler_params=_params(("parallel",)),
        name="merge",
    )(x2, g, wg, bg, o_sb, wsb, o_ret, wret, wo)


def _ff_chunks():
    chunks, lo = [], 0
    while lo < D_FF:
        w = min(FF_CHUNK, D_FF - lo)
        chunks.append((lo, w))
        lo += w
    return chunks


def _ffn_kernel(x_ref, xp_ref, g_ref, wu_ref, cw_ref, cb_ref, wd_ref, fg_ref, o_ref, abuf_ref,
                *, tm, tiles_per_seq, final):
    i = pl.program_id(0)
    g = g_ref[...]
    x = x_ref[...]
    h = _rms(x, g).astype(BF16)
    keep = jnp.where(lax.rem(i, tiles_per_seq) == 0, 0.0, 1.0)
    hp = (_rms(xp_ref[...], g) * keep).astype(BF16)
    chunks = _ff_chunks()

    def up(c):
        lo, w = chunks[c]
        wa = wu_ref[:, lo:lo + w]
        a = jnp.dot(h, wa, preferred_element_type=F32)
        b = jnp.dot(h, wu_ref[:, D_FF + lo:D_FF + lo + w], preferred_element_type=F32)
        ap = jnp.dot(hp, wa, preferred_element_type=F32)
        return a, b, ap

    nxt = up(0)
    y = x
    for c, (lo, w) in enumerate(chunks):
        a, b, ap = nxt
        if c + 1 < len(chunks):
            nxt = up(c + 1)
        buf = abuf_ref.at[c % 2]
        buf[0:HALO, 0:w] = ap
        buf[HALO:, 0:w] = a
        cw = cw_ref[:, lo:lo + w]
        ext = buf[:, 0:w]
        conv = (pltpu.roll(ext, 2, axis=0)[HALO:] * cw[0:1, :]
                + pltpu.roll(ext, 1, axis=0)[HALO:] * cw[1:2, :]
                + a * cw[2:3, :] + cb_ref[:, lo:lo + w])
        act = (conv * jax.nn.sigmoid(conv) * b).astype(BF16)
        y = y + jnp.dot(act, wd_ref[lo:lo + w, :], preferred_element_type=F32)
    if final:
        y = _rms(y, fg_ref[...])
    o_ref[...] = y


def _conv_ffn(x2, g, w_up, cw, cb, wd, fg, *, layer, tm, seq, final):
    t = x2.shape[0]
    hb = tm // HALO
    kern = functools.partial(_ffn_kernel, tm=tm, tiles_per_seq=seq // tm, final=final)
    resident = dict(pipeline_mode=pl.Buffered(1))
    return pl.pallas_call(
        kern,
        grid=(t // tm,),
        in_specs=[
            pl.BlockSpec((tm, D_MODEL), lambda i: (i, 0)),
            pl.BlockSpec((HALO, D_MODEL), lambda i: (jnp.maximum(i * hb - 1, 0), 0)),
            _layer_spec((1, D_MODEL), layer),
            _layer_spec((D_MODEL, 2 * D_FF), layer, **resident),
            _layer_spec((CONV_WIDTH, D_FF), layer),
            _layer_spec((1, D_FF), layer),
            _layer_spec((D_FF, D_MODEL), layer, **resident),
            pl.BlockSpec((1, D_MODEL), lambda i: (0, 0)),
        ],
        out_specs=pl.BlockSpec((tm, D_MODEL), lambda i: (i, 0)),
        out_shape=jax.ShapeDtypeStruct((t, D_MODEL), F32),
        scratch_shapes=[pltpu.VMEM((2, tm + HALO, FF_CHUNK), F32)],
        compiler_params=_params(("parallel",)),
        name="conv_ffn",
    )(x2, x2, g, w_up, cw, cb, wd, fg)


def kernel(x, attn_norm, w_in, b_gate, sb_w_out, ret_norm, ret_w_out, w_o, ffn_norm, w_up, conv_w, conv_b,
           w_down, final_norm):
    batch, seq, d = x.shape
    assert d == D_MODEL
    depth = w_in.shape[0]
    t = batch * seq
    tm = min(512, seq)
    assert seq % tm == 0 and tm % HALO == 0

    half = RET_QK_DIM // 2
    inv = ROPE_BASE ** (-jnp.arange(half, dtype=F32) / half)
    ang = jnp.arange(seq, dtype=F32)[:, None] * inv[None, :]
    cosf = jnp.concatenate([jnp.cos(ang), jnp.cos(ang)], axis=-1)
    sins = jnp.concatenate([-jnp.sin(ang), jnp.sin(ang)], axis=-1)

    ch = min(RET_CHUNK, seq)
    tables = _retention_tables(ch)
    blk = min(SB_BLOCK, seq)
    idx = jnp.arange(blk)
    tri = (idx[:, None] >= idx[None, :]).astype(BF16)

    def rows(p):
        return p.reshape(depth, 1, p.shape[-1])

    w_in_bf, w_up_bf, w_down_bf = w_in.astype(BF16), w_up.astype(BF16), w_down.astype(BF16)
    w_gate_bf = w_in_bf[:, :, PROJ_COLS:]
    sb_w_bf, ret_w_bf, w_o_bf = sb_w_out.astype(BF16), ret_w_out.astype(BF16), w_o.astype(BF16)
    g_attn, g_ffn, g_ret, bg, cb = rows(attn_norm), rows(ffn_norm), rows(ret_norm), rows(b_gate), rows(conv_b)

    x2 = x.reshape(t, d)
    for l in range(depth):
        proj_bf, o_ret = _in_proj(x2, g_attn, w_in_bf, cosf, sins, g_ret, tables, layer=l, tm=tm, seq=seq)
        o_sb = _sb_attention(proj_bf, tri, batch=batch, seq=seq)
        x2 = _merge(x2, g_attn, w_gate_bf, bg, o_sb, sb_w_bf, o_ret, ret_w_bf, w_o_bf, layer=l, tm=tm)
        x2 = _conv_ffn(x2, g_ffn, w_up_bf, conv_w, cb, w_down_bf, final_norm[None, :], layer=l, tm=tm, seq=seq,
                       final=(l == depth - 1))
    return x2.reshape(batch, seq, d)
```
